```python
import math
import jax, jax.numpy as jnp
from jax import lax
import numpy as np

D_MODEL = 1024
BATCH = 2
SEQ = 8192
DEPTH = 2
DEC_BATCH = 32
DEC_SEQ = 4
PAST_LEN = 8192
PAGE_SIZE = 128

HEAD_DIM = 64
H_FOX = 6
H_SB = 6
H_DIFF = 4
DIFF_DIM = HEAD_DIM // 2
MIX_W = (H_FOX + H_SB + H_DIFF) * HEAD_DIM
D_FF = 2816
N_BUCKETS = 32
MAX_DISTANCE = 128
Q_BLOCK = 128
EPS = 1e-6
PROJ_SIZES = (H_FOX * HEAD_DIM, H_FOX * HEAD_DIM, H_FOX * HEAD_DIM, H_FOX,
              H_SB * HEAD_DIM, H_SB * HEAD_DIM, H_SB * HEAD_DIM,
              H_DIFF * HEAD_DIM, H_DIFF * HEAD_DIM, H_DIFF * HEAD_DIM)
IN_W = sum(PROJ_SIZES)

kernel_name = 'hybrid_fox_stickbreak_diff_decode_step'


def rms_norm(x, g):
    xf = x.astype(jnp.float32)
    y = xf * lax.rsqrt(jnp.mean(xf * xf, axis=-1, keepdims=True) + EPS)
    return (y * g.astype(jnp.float32)).astype(x.dtype)


def swiglu_ffn(x, g, w_gate, w_up, w_down):
    h = rms_norm(x, g)
    return (jax.nn.silu(h @ w_gate) * (h @ w_up)) @ w_down


def rel_bucket(rel):
    n = jnp.maximum(-rel, 0)
    max_exact = N_BUCKETS // 2
    nf = jnp.maximum(n, 1).astype(jnp.float32)
    large = max_exact + (jnp.log(nf / max_exact) / math.log(MAX_DISTANCE / max_exact)
                         * (N_BUCKETS - max_exact)).astype(jnp.int32)
    large = jnp.minimum(large, N_BUCKETS - 1)
    return jnp.where(n < max_exact, n, large)


def project(h, w_in, b_forget):
    n, t, _ = h.shape
    p = h @ w_in
    offs = np.cumsum(PROJ_SIZES)[:-1].tolist()
    qa, ka, va, fa, qb, kb, vb, qc, kc, vc = jnp.split(p, offs, axis=-1)
    heads = lambda a, nh: a.reshape(n, t, nh, HEAD_DIM)
    logf = jax.nn.log_sigmoid((fa + b_forget).astype(jnp.float32))
    return (heads(qa, H_FOX), heads(ka, H_FOX), heads(va, H_FOX), logf,
            heads(qb, H_SB), heads(kb, H_SB), heads(vb, H_SB),
            heads(qc, H_DIFF), heads(kc, H_DIFF), heads(vc, H_DIFF))


def attend(qa, ka, va, cfq, cfk, qb, kb, vb, qc, kc, vc, pos_q, pos_k,
           lam, lam_init, diff_gain, rel_bias):
    f32 = jnp.float32
    neg = jnp.finfo(f32).min
    n, tq = qa.shape[:2]
    causal = pos_k[None, :] <= pos_q[:, None]
    strict = pos_k[None, :] < pos_q[:, None]
    s = jnp.einsum('nqhd,nkhd->nhqk', qa, ka, preferred_element_type=f32) * HEAD_DIM ** -0.5
    s = s + jnp.transpose(cfq, (0, 2, 1))[..., :, None] - jnp.transpose(cfk, (0, 2, 1))[..., None, :]
    p = jax.nn.softmax(jnp.where(causal, s, neg), axis=-1)
    o_a = jnp.einsum('nhqk,nkhd->nqhd', p.astype(va.dtype), va)
    z = jnp.einsum('nqhd,nkhd->nhqk', qb, kb, preferred_element_type=f32) * HEAD_DIM ** -0.5
    sp = jnp.where(strict, jax.nn.softplus(z), 0.0)
    later = lax.cumsum(sp, axis=3, reverse=True) - sp
    w = jnp.where(strict, jnp.exp(jax.nn.log_sigmoid(z) - later), 0.0)
    o_b = jnp.einsum('nhqk,nkhd->nqhd', w.astype(vb.dtype), vb)
    bias = rel_bias[rel_bucket(pos_k[None, :] - pos_q[:, None])]
    bias = jnp.transpose(bias, (2, 0, 1)).astype(f32)
    sc = DIFF_DIM ** -0.5
    q1, q2 = qc[..., :DIFF_DIM], qc[..., DIFF_DIM:]
    k1, k2 = kc[..., :DIFF_DIM], kc[..., DIFF_DIM:]
    s1 = jnp.einsum('nqhd,nkhd->nhqk', q1, k1, preferred_element_type=f32) * sc + bias
    s2 = jnp.einsum('nqhd,nkhd->nhqk', q2, k2, preferred_element_type=f32) * sc + bias
    p1 = jax.nn.softmax(jnp.where(causal, s1, neg), axis=-1)
    p2 = jax.nn.softmax(jnp.where(causal, s2, neg), axis=-1)
    o_c = jnp.einsum('nhqk,nkhd->nqhd', (p1 - lam * p2).astype(vc.dtype), vc)
    o_c = rms_norm(o_c, diff_gain) * (1.0 - lam_init)
    return jnp.concatenate([o_a.reshape(n, tq, -1), o_b.reshape(n, tq, -1),
                            o_c.reshape(n, tq, -1)], axis=-1)


def mix_prompt(h, w_in, b_forget, w_out, lam, lam_init, diff_gain, rel_bias):
    n, t, _ = h.shape
    qa, ka, va, logf, qb, kb, vb, qc, kc, vc = project(h, w_in, b_forget)
    cf = jnp.cumsum(logf, axis=1)
    pos_k = jnp.arange(t, dtype=jnp.int32)

    def block(b):
        start = b * Q_BLOCK
        sl = lambda a: lax.dynamic_slice_in_dim(a, start, Q_BLOCK, axis=1)
        pos_q = start + jnp.arange(Q_BLOCK, dtype=jnp.int32)
        return attend(sl(qa), ka, va, sl(cf), cf, sl(qb), kb, vb, sl(qc), kc, vc,
                      pos_q, pos_k, lam, lam_init, diff_gain, rel_bias)

    o = lax.map(block, jnp.arange(t // Q_BLOCK, dtype=jnp.int32))
    o = jnp.transpose(o, (1, 0, 2, 3)).reshape(n, t, MIX_W)
    rows = (jnp.stack([ka, va], axis=2), logf, jnp.stack([kb, vb], axis=2), jnp.stack([kc, vc], axis=2))
    return o @ w_out, rows


def gather_pages(cache_l, page_table):
    g = cache_l[page_table]
    return g.reshape((g.shape[0], g.shape[1] * g.shape[2]) + g.shape[3:])


def mix_sample(h, fox_kv_l, fox_logf_l, sb_kv_l, diff_kv_l, page_table,
               w_in, b_forget, w_out, lam, lam_init, diff_gain, rel_bias):
    n, t, _ = h.shape
    qa, ka, va, logf, qb, kb, vb, qc, kc, vc = project(h, w_in, b_forget)
    past_fox = gather_pages(fox_kv_l, page_table)
    past_sb = gather_pages(sb_kv_l, page_table)
    past_diff = gather_pages(diff_kv_l, page_table)
    past_logf = gather_pages(fox_logf_l, page_table).astype(jnp.float32)
    past_len = past_fox.shape[1]
    cat = lambda past, new: jnp.concatenate([past, new], axis=1)
    cf_all = jnp.cumsum(cat(past_logf, logf), axis=1)
    pos_q = past_len + jnp.arange(t, dtype=jnp.int32)
    pos_k = jnp.arange(past_len + t, dtype=jnp.int32)
    o = attend(qa, cat(past_fox[:, :, 0], ka), cat(past_fox[:, :, 1], va), cf_all[:, past_len:], cf_all,
               qb, cat(past_sb[:, :, 0], kb), cat(past_sb[:, :, 1], vb),
               qc, cat(past_diff[:, :, 0], kc), cat(past_diff[:, :, 1], vc),
               pos_q, pos_k, lam, lam_init, diff_gain, rel_bias)
    rows = (jnp.stack([ka, va], axis=2), logf, jnp.stack([kb, vb], axis=2), jnp.stack([kc, vc], axis=2))
    return o @ w_out, rows


def setup_inputs(seed: int = 0) -> dict:
    key = jax.random.key(seed)
    ks = iter(jax.random.split(key, 40))
    nrm = lambda shape, scale=1.0: jax.random.normal(next(ks), shape, jnp.float32) * scale
    n_pages = PAST_LEN // PAGE_SIZE
    n_used = DEC_BATCH * n_pages
    n_phys = (5 * n_used + 3) // 4
    page_table = jax.random.permutation(next(ks), n_phys)[:n_used].reshape(DEC_BATCH, n_pages).astype(jnp.int32)
    gain = lambda shape: 1.0 + nrm(shape, 0.01)
    return {
        'x_prompt': nrm((BATCH, SEQ, D_MODEL)),
        'x_sample': nrm((DEC_BATCH, DEC_SEQ, D_MODEL)),
        'cache_fox_kv': nrm((DEPTH, n_phys, PAGE_SIZE, 2, H_FOX, HEAD_DIM)),
        'cache_fox_logf': jax.nn.log_sigmoid(3.0 + nrm((DEPTH, n_phys, PAGE_SIZE, H_FOX))),
        'cache_sb_kv': nrm((DEPTH, n_phys, PAGE_SIZE, 2, H_SB, HEAD_DIM)),
        'cache_diff_kv': nrm((DEPTH, n_phys, PAGE_SIZE, 2, H_DIFF, HEAD_DIM)),
        'page_table': page_table,
        'norm_ffn1': gain((DEPTH, D_MODEL)),
        'ffn1_gate': nrm((DEPTH, D_MODEL, D_FF), D_MODEL ** -0.5),
        'ffn1_up': nrm((DEPTH, D_MODEL, D_FF), D_MODEL ** -0.5),
        'ffn1_down': nrm((DEPTH, D_FF, D_MODEL), D_FF ** -0.5),
        'norm_mix': gain((DEPTH, D_MODEL)),
        'w_in': nrm((DEPTH, D_MODEL, IN_W), D_MODEL ** -0.5),
        'b_forget': 3.0 + nrm((DEPTH, H_FOX), 0.1),
        'lambda_q1': nrm((DEPTH, DIFF_DIM), 0.1),
        'lambda_k1': nrm((DEPTH, DIFF_DIM), 0.1),
        'lambda_q2': nrm((DEPTH, DIFF_DIM), 0.1),
        'lambda_k2': nrm((DEPTH, DIFF_DIM), 0.1),
        'diff_norm': gain((DEPTH, HEAD_DIM)),
        'w_out': nrm((DEPTH, MIX_W, D_MODEL), MIX_W ** -0.5),
        'rel_bias': nrm((N_BUCKETS, H_DIFF), 0.5),
        'norm_ffn2': gain((DEPTH, D_MODEL)),
        'ffn2_gate': nrm((DEPTH, D_MODEL, D_FF), D_MODEL ** -0.5),
        'ffn2_up': nrm((DEPTH, D_MODEL, D_FF), D_MODEL ** -0.5),
        'ffn2_down': nrm((DEPTH, D_FF, D_MODEL), D_FF ** -0.5),
        'final_norm': gain((D_MODEL,)),
    }


def reference(x_prompt, x_sample, cache_fox_kv, cache_fox_logf, cache_sb_kv, cache_diff_kv, page_table,
              norm_ffn1, ffn1_gate, ffn1_up, ffn1_down, norm_mix, w_in, b_forget,
              lambda_q1, lambda_k1, lambda_q2, lambda_k2, diff_norm, w_out, rel_bias,
              norm_ffn2, ffn2_gate, ffn2_up, ffn2_down, final_norm):
    f32 = jnp.float32
    xp, xs = x_prompt, x_sample
    fkv_p, fkv_s, flf_p, flf_s, sb_p, sb_s, df_p, df_s = [], [], [], [], [], [], [], []
    for l in range(DEPTH):
        lam_init = 0.8 - 0.6 * math.exp(-0.3 * l)
        lam = (jnp.exp(jnp.sum((lambda_q1[l] * lambda_k1[l]).astype(f32)))
               - jnp.exp(jnp.sum((lambda_q2[l] * lambda_k2[l]).astype(f32))) + lam_init)
        xp = xp + 0.5 * swiglu_ffn(xp, norm_ffn1[l], ffn1_gate[l], ffn1_up[l], ffn1_down[l])
        xs = xs + 0.5 * swiglu_ffn(xs, norm_ffn1[l], ffn1_gate[l], ffn1_up[l], ffn1_down[l])
        mp, rp = mix_prompt(rms_norm(xp, norm_mix[l]), w_in[l], b_forget[l], w_out[l],
                            lam, lam_init, diff_norm[l], rel_bias)
        ms, rs = mix_sample(rms_norm(xs, norm_mix[l]), cache_fox_kv[l], cache_fox_logf[l],
                            cache_sb_kv[l], cache_diff_kv[l], page_table,
                            w_in[l], b_forget[l], w_out[l], lam, lam_init, diff_norm[l], rel_bias)
        xp = xp + mp
        xs = xs + ms
        xp = xp + 0.5 * swiglu_ffn(xp, norm_ffn2[l], ffn2_gate[l], ffn2_up[l], ffn2_down[l])
        xs = xs + 0.5 * swiglu_ffn(xs, norm_ffn2[l], ffn2_gate[l], ffn2_up[l], ffn2_down[l])
        fkv_p.append(rp[0]); flf_p.append(rp[1]); sb_p.append(rp[2]); df_p.append(rp[3])
        fkv_s.append(rs[0]); flf_s.append(rs[1]); sb_s.append(rs[2]); df_s.append(rs[3])
    y_prompt = rms_norm(xp, final_norm)
    y_sample = rms_norm(xs, final_norm)
    return (y_prompt, y_sample,
            jnp.stack(fkv_p), jnp.stack(fkv_s), jnp.stack(flf_p), jnp.stack(flf_s),
            jnp.stack(sb_p), jnp.stack(sb_s), jnp.stack(df_p), jnp.stack(df_s))
```

```python
import functools
import math

import jax
import jax.numpy as jnp
import numpy as np
from jax import lax
from jax.experimental import pallas as pl
from jax.experimental.pallas import tpu as pltpu

F32 = jnp.float32
BF16 = jnp.bfloat16

HEAD_DIM = 64
H_FOX = 6
H_SB = 6
H_DIFF = 4
DIFF_DIM = HEAD_DIM // 2
N_BUCKETS = 32
MAX_DISTANCE = 128
EPS = 1e-6

LANES = 128
W_FOX = H_FOX * HEAD_DIM
W_SB = H_SB * HEAD_DIM
W_DIFF = H_DIFF * HEAD_DIM
FF_CHUNK = 256
NEG = -1e30
VMEM_LIMIT = 56 * 1024 * 1024


def _rms(x, g):
    ms = jnp.mean(x * x, axis=-1, keepdims=True)
    return x * lax.rsqrt(ms + EPS) * g


def _nt_dot(a, b):
    return lax.dot_general(a, b, (((1,), (1,)), ((), ())), preferred_element_type=F32)


def _dot(a, b):
    return jnp.dot(a, b, preferred_element_type=F32)


def _const_spec(shape):
    nd = len(shape)
    return pl.BlockSpec(shape, lambda *_: (0,) * nd, pipeline_mode=pl.Buffered(1))


def _swiglu(x, g_ref, wg_ref, wu_ref, wd_ref):
    h = _rms(x, g_ref[...]).astype(BF16)
    acc = jnp.zeros_like(x)
    for c in range(wg_ref.shape[0]):
        gate = _dot(h, wg_ref[c])
        up = _dot(h, wu_ref[c])
        a = (gate * jax.nn.sigmoid(gate) * up).astype(BF16)
        acc = acc + _dot(a, wd_ref[c])
    return x + 0.5 * acc


def _ffn_kernel(x_ref, g_ref, wg_ref, wu_ref, wd_ref, o_ref):
    o_ref[...] = _swiglu(x_ref[...], g_ref, wg_ref, wu_ref, wd_ref)


def _mix_ffn_kernel(x_ref, oa_ref, ob_ref, oc_ref, woa_ref, wob_ref, woc_ref,
                    g_ref, wg_ref, wu_ref, wd_ref, fg_ref, o_ref, *, final):
    x = x_ref[...]
    x = x + _dot(oa_ref[...], woa_ref[...]) + _dot(ob_ref[...], wob_ref[...]) + _dot(oc_ref[...], woc_ref[...])
    y = _swiglu(x, g_ref, wg_ref, wu_ref, wd_ref)
    if final:
        y = _rms(y, fg_ref[...])
    o_ref[...] = y


def _token_tile(rows):
    return 512 if rows % 512 == 0 else rows


def _ffn_call(x, g, wg, wu, wd):
    rows, d = x.shape
    tm = _token_tile(rows)
    row_spec = pl.BlockSpec((tm, d), lambda i: (i, 0))
    return pl.pallas_call(
        _ffn_kernel,
        grid=(rows // tm,),
        in_specs=[row_spec, _const_spec(g.shape), _const_spec(wg.shape), _const_spec(wu.shape),
                  _const_spec(wd.shape)],
        out_specs=row_spec,
        out_shape=jax.ShapeDtypeStruct((rows, d), F32),
        compiler_params=pltpu.CompilerParams(dimension_semantics=("arbitrary",),
                                             vmem_limit_bytes=VMEM_LIMIT),
        name="ffn",
    )(x, g, wg, wu, wd)


def _mix_ffn_call(x, oa, ob, oc, woa, wob, woc, g, wg, wu, wd, fg, final):
    rows, d = x.shape
    tm = _token_tile(rows)
    row = lambda w: pl.BlockSpec((tm, w), lambda i: (i, 0))
    return pl.pallas_call(
        functools.partial(_mix_ffn_kernel, final=final),
        grid=(rows // tm,),
        in_specs=[row(d), row(W_FOX), row(W_SB), row(W_DIFF),
                  _const_spec(woa.shape), _const_spec(wob.shape), _const_spec(woc.shape),
                  _const_spec(g.shape), _const_spec(wg.shape), _const_spec(wu.shape), _const_spec(wd.shape),
                  _const_spec(fg.shape)],
        out_specs=row(d),
        out_shape=jax.ShapeDtypeStruct((rows, d), F32),
        compiler_params=pltpu.CompilerParams(dimension_semantics=("arbitrary",),
                                             vmem_limit_bytes=VMEM_LIMIT),
        name="mix_ffn",
    )(x, oa, ob, oc, woa, wob, woc, g, wg, wu, wd, fg)


def _split3(x):
    hi = x.astype(BF16)
    r1 = x - hi.astype(F32)
    mid = r1.astype(BF16)
    lo = (r1 - mid.astype(F32)).astype(BF16)
    return hi, mid, lo


def _pair_masked_store(q, n_heads, seg, scale, out_ref):
    lane = lax.broadcasted_iota(jnp.int32, (q.shape[0], LANES), 1)
    per_pair = LANES // seg
    for j in range(n_heads * HEAD_DIM // LANES):
        q2 = q[:, LANES * j:LANES * (j + 1)] * scale
        for t in range(per_pair):
            keep = (lane >= seg * t) & (lane < seg * (t + 1))
            v = per_pair * j + t
            out_ref[0, :, LANES * v:LANES * (v + 1)] = jnp.where(keep, q2, 0.0).astype(BF16)


def _proj_kernel(x_ref, g_ref, wa_ref, wf_ref, bf_ref, wb_ref, wc_ref, tri_ref,
                 qxa_ref, ka_ref, va_ref, kva_ref, logf_ref, cf_ref,
                 qxb_ref, kb_ref, vb_ref, kvb_ref,
                 qxc_ref, kc_ref, vc_ref, kvc_ref, carry_ref):
    h = _rms(x_ref[0], g_ref[...]).astype(BF16)

    pa = _dot(h, wa_ref[...])
    _pair_masked_store(pa[:, :W_FOX], H_FOX, HEAD_DIM, HEAD_DIM ** -0.5, qxa_ref)
    ka_ref[0] = pa[:, W_FOX:2 * W_FOX].astype(BF16)
    va_ref[0] = pa[:, 2 * W_FOX:].astype(BF16)
    kva_ref[0] = pa[:, W_FOX:]

    logf = jax.nn.log_sigmoid(_dot(h, wf_ref[...]) + bf_ref[...])
    logf_ref[0] = logf

    @pl.when(pl.program_id(1) == 0)
    def _():
        carry_ref[...] = jnp.zeros_like(carry_ref)

    hi, mid, lo = _split3(logf)
    tri = tri_ref[...]
    cf = _dot(tri, hi) + _dot(tri, mid) + _dot(tri, lo) + carry_ref[0:1, :]
    cf_ref[0] = cf
    carry_ref[...] = jnp.broadcast_to(cf[cf.shape[0] - 1:, :], carry_ref.shape)

    pb = _dot(h, wb_ref[...])
    _pair_masked_store(pb[:, :W_SB], H_SB, HEAD_DIM, HEAD_DIM ** -0.5, qxb_ref)
    kb_ref[0] = pb[:, W_SB:2 * W_SB].astype(BF16)
    vb_ref[0] = pb[:, 2 * W_SB:].astype(BF16)
    kvb_ref[0] = pb[:, W_SB:]

    pc = _dot(h, wc_ref[...])
    _pair_masked_store(pc[:, :W_DIFF], H_DIFF, DIFF_DIM, DIFF_DIM ** -0.5, qxc_ref)
    kc_ref[0] = pc[:, W_DIFF:2 * W_DIFF].astype(BF16)
    vc_ref[0] = pc[:, 2 * W_DIFF:].astype(BF16)
    kvc_ref[0] = pc[:, W_DIFF:]


def _proj_call(x, g, wa, wf, bf, wb, wc):
    n, t, d = x.shape
    tm = _token_tile(t)
    tri = jnp.tril(jnp.ones((tm, tm), BF16))
    blk = lambda w: pl.BlockSpec((1, tm, w), lambda b, i: (b, i, 0))
    out = lambda w, dt: jax.ShapeDtypeStruct((n, t, w), dt)
    widths = [(2 * W_FOX, BF16), (W_FOX, BF16), (W_FOX, BF16), (2 * W_FOX, F32), (LANES, F32), (LANES, F32),
              (2 * W_SB, BF16), (W_SB, BF16), (W_SB, BF16), (2 * W_SB, F32),
              (4 * W_DIFF, BF16), (W_DIFF, BF16), (W_DIFF, BF16), (2 * W_DIFF, F32)]
    return pl.pallas_call(
        _proj_kernel,
        grid=(n, t // tm),
        in_specs=[blk(d), _const_spec(g.shape), _const_spec(wa.shape), _const_spec(wf.shape),
                  _const_spec(bf.shape), _const_spec(wb.shape), _const_spec(wc.shape), _const_spec(tri.shape)],
        out_specs=[blk(w) for w, _ in widths],
        out_shape=[out(w, dt) for w, dt in widths],
        scratch_shapes=[pltpu.VMEM((8, LANES), F32)],
        compiler_params=pltpu.CompilerParams(dimension_semantics=("arbitrary", "arbitrary"),
                                             vmem_limit_bytes=VMEM_LIMIT),
        name="proj",
    )(x, g, wa, wf, bf, wb, wc, tri)


def _causal_pairs(nq):
    qs, ks = [], []
    for q in range(nq):
        for k in range(q, -1, -1):
            qs.append(q)
            ks.append(k)
    return jnp.asarray(qs, jnp.int32), jnp.asarray(ks, jnp.int32)


def _block_delta(bq, bk):
    return lax.broadcasted_iota(jnp.int32, (bq, bk), 1) - lax.broadcasted_iota(jnp.int32, (bq, bk), 0)


def _softmax_step(s, h, m_ref, l_ref, acc_ref, v):
    m_prev = m_ref[h]
    m_new = jnp.maximum(m_prev, jnp.max(s, axis=-1, keepdims=True))
    alpha = jnp.exp(m_prev - m_new)
    p = jnp.exp(s - m_new[:, :1])
    l_ref[h] = alpha * l_ref[h] + jnp.sum(p, axis=-1, keepdims=True)
    acc_ref[h] = alpha * acc_ref[h] + _dot(p.astype(BF16), v)
    m_ref[h] = m_new


def _fox_kernel(qt_ref, kt_ref, qx_ref, k_ref, v_ref, cfq_ref, cfk_ref, o_ref, m_ref, l_ref, acc_ref, *, blk):
    p_id = pl.program_id(1)
    qi, ki = qt_ref[p_id], kt_ref[p_id]

    @pl.when(ki == qi)
    def _():
        m_ref[...] = jnp.full_like(m_ref, NEG)
        l_ref[...] = jnp.zeros_like(l_ref)
        acc_ref[...] = jnp.zeros_like(acc_ref)

    causal = _block_delta(blk, blk) <= (qi - ki) * blk
    cfq = cfq_ref[0]
    cfk = cfk_ref[0]
    for h in range(H_FOX):
        j = h // 2
        s = _nt_dot(qx_ref[0, :, LANES * h:LANES * (h + 1)], k_ref[0, :, LANES * j:LANES * (j + 1)])
        s = s + cfq[:, h:h + 1] - cfk[h:h + 1, :]
        s = jnp.where(causal, s, NEG)
        _softmax_step(s, h, m_ref, l_ref, acc_ref, v_ref[0, :, LANES * j:LANES * (j + 1)])

    @pl.when(ki == 0)
    def _():
        lane = lax.broadcasted_iota(jnp.int32, (blk, LANES), 1)
        for j in range(H_FOX // 2):
            o = jnp.where(lane < HEAD_DIM, acc_ref[2 * j] / l_ref[2 * j], acc_ref[2 * j + 1] / l_ref[2 * j + 1])
            o_ref[0, :, LANES * j:LANES * (j + 1)] = o.astype(BF16)


def _sb_kernel(qt_ref, kt_ref, qx_ref, k_ref, v_ref, u_ref, o_ref, carry_ref, acc_ref, *, blk):
    p_id = pl.program_id(1)
    qi, ki = qt_ref[p_id], kt_ref[p_id]

    @pl.when(ki == qi)
    def _():
        carry_ref[...] = jnp.zeros_like(carry_ref)
        acc_ref[...] = jnp.zeros_like(acc_ref)

    strict = _block_delta(blk, blk) < (qi - ki) * blk
    u = u_ref[...]
    for h in range(H_SB):
        j = h // 2
        z = _nt_dot(qx_ref[0, :, LANES * h:LANES * (h + 1)], k_ref[0, :, LANES * j:LANES * (j + 1)])
        sp = jnp.maximum(z, 0.0) + jnp.log(1.0 + jnp.exp(-jnp.abs(z)))
        sp = jnp.where(strict, sp, 0.0)
        later = _dot(sp.astype(BF16), u)
        carry = carry_ref[h]
        w = jnp.where(strict, jnp.exp((z - sp) - later - carry[:, :1]), 0.0)
        acc_ref[h] = acc_ref[h] + _dot(w.astype(BF16), v_ref[0, :, LANES * j:LANES * (j + 1)])
        carry_ref[h] = carry + (later[:, :1] + sp[:, :1])

    @pl.when(ki == 0)
    def _():
        lane = lax.broadcasted_iota(jnp.int32, (blk, LANES), 1)
        for j in range(H_SB // 2):
            o = jnp.where(lane < HEAD_DIM, acc_ref[2 * j], acc_ref[2 * j + 1])
            o_ref[0, :, LANES * j:LANES * (j + 1)] = o.astype(BF16)


def _lambda_value(lp_ref, lam_init):
    lp = lp_ref[...]
    s1 = jnp.sum(lp[0:1] * lp[1:2], axis=-1, keepdims=True)
    s2 = jnp.sum(lp[2:3] * lp[3:4], axis=-1, keepdims=True)
    return jnp.exp(s1) - jnp.exp(s2) + lam_init


def _diff_finish(o1, o2, lam, gain, lam_init):
    lane = lax.broadcasted_iota(jnp.int32, o1.shape, 1)
    left = lane < HEAD_DIM
    o = o1 - lam * o2
    sq = o * o
    ss_all = jnp.sum(sq, axis=-1, keepdims=True)
    ss_left = jnp.sum(jnp.where(left, sq, 0.0), axis=-1, keepdims=True)
    ms = jnp.where(left, ss_left, ss_all - ss_left) * (1.0 / HEAD_DIM)
    return o * lax.rsqrt(ms + EPS) * gain * (1.0 - lam_init)


def _diff_kernel(qt_ref, kt_ref, qx_ref, k_ref, v_ref, bias_ref, lp_ref, gain_ref, o_ref,
                 m_ref, l_ref, acc_ref, *, blk, lam_init):
    p_id = pl.program_id(1)
    qi, ki = qt_ref[p_id], kt_ref[p_id]

    @pl.when(ki == qi)
    def _():
        m_ref[...] = jnp.full_like(m_ref, NEG)
        l_ref[...] = jnp.zeros_like(l_ref)
        acc_ref[...] = jnp.zeros_like(acc_ref)

    causal = _block_delta(blk, blk) <= (qi - ki) * blk
    for t in range(2 * H_DIFF):
        h = t // 2
        j = h // 2
        s = _nt_dot(qx_ref[0, :, LANES * t:LANES * (t + 1)], k_ref[0, :, LANES * j:LANES * (j + 1)])
        s = jnp.where(causal, s + bias_ref[0, h], NEG)
        _softmax_step(s, t, m_ref, l_ref, acc_ref, v_ref[0, :, LANES * j:LANES * (j + 1)])

    @pl.when(ki == 0)
    def _():
        lam = _lambda_value(lp_ref, lam_init)
        lane = lax.broadcasted_iota(jnp.int32, (blk, LANES), 1)
        left = lane < HEAD_DIM
        for j in range(H_DIFF // 2):
            ta, tb = 4 * j, 4 * j + 2
            o1 = jnp.where(left, acc_ref[ta] / l_ref[ta], acc_ref[tb] / l_ref[tb])
            o2 = jnp.where(left, acc_ref[ta + 1] / l_ref[ta + 1], acc_ref[tb + 1] / l_ref[tb + 1])
            o = _diff_finish(o1, o2, lam, gain_ref[...], lam_init)
            o_ref[0, :, LANES * j:LANES * (j + 1)] = o.astype(BF16)


def _bucket_bias(n, rb_ref, h):
    max_exact = N_BUCKETS // 2
    nf = jnp.maximum(n, 1).astype(F32)
    large = max_exact + (jnp.log(nf / max_exact) / math.log(MAX_DISTANCE / max_exact)
                         * (N_BUCKETS - max_exact)).astype(jnp.int32)
    bucket = jnp.where(n < max_exact, n, jnp.minimum(large, N_BUCKETS - 1))
    far = rb_ref[N_BUCKETS - 1, h]
    acc = jnp.zeros(n.shape, F32)
    for b in range(N_BUCKETS - 1):
        acc = jnp.where(bucket == b, rb_ref[b, h] - far, acc)
    return acc


def _bias_kernel(rb_ref, o_ref, *, blk):
    off = pl.program_id(0)
    h = pl.program_id(1)
    n = jnp.maximum(off * blk - _block_delta(blk, blk), 0)
    o_ref[0, 0] = _bucket_bias(n, rb_ref, h)


def _bias_call(rel_bias, blk):
    return pl.pallas_call(
        functools.partial(_bias_kernel, blk=blk),
        grid=(3, H_DIFF),
        in_specs=[pl.BlockSpec(memory_space=pltpu.SMEM)],
        out_specs=pl.BlockSpec((1, 1, blk, blk), lambda o, h: (o, h, 0, 0)),
        out_shape=jax.ShapeDtypeStruct((3, H_DIFF, blk, blk), F32),
        name="bias_tiles",
    )(rel_bias)


def _attn_block(t):
    return 512 if t % 512 == 0 else t


def _attn_params():
    return pltpu.CompilerParams(dimension_semantics=("arbitrary", "arbitrary"), vmem_limit_bytes=VMEM_LIMIT)


def _fox_call(qx, k, v, cf, cf_t):
    n, t, _ = k.shape
    blk = _attn_block(t)
    qt, kt = _causal_pairs(t // blk)
    qmap = lambda b, p, qt, kt: (b, qt[p], 0)
    kmap = lambda b, p, qt, kt: (b, kt[p], 0)
    grid_spec = pltpu.PrefetchScalarGridSpec(
        num_scalar_prefetch=2, grid=(n, qt.shape[0]),
        in_specs=[pl.BlockSpec((1, blk, 2 * W_FOX), qmap), pl.BlockSpec((1, blk, W_FOX), kmap),
                  pl.BlockSpec((1, blk, W_FOX), kmap), pl.BlockSpec((1, blk, LANES), qmap),
                  pl.BlockSpec((1, 8, blk), lambda b, p, qt, kt: (b, 0, kt[p]))],
        out_specs=pl.BlockSpec((1, blk, W_FOX), qmap),
        scratch_shapes=[pltpu.VMEM((H_FOX, blk, LANES), F32)] * 3)
    return pl.pallas_call(
        functools.partial(_fox_kernel, blk=blk), grid_spec=grid_spec,
        out_shape=jax.ShapeDtypeStruct((n, t, W_FOX), BF16),
        compiler_params=_attn_params(), name="fox_attn",
    )(qt, kt, qx, k, v, cf, cf_t)


def _sb_call(qx, k, v):
    n, t, _ = k.shape
    blk = _attn_block(t)
    qt, kt = _causal_pairs(t // blk)
    u = jnp.tril(jnp.ones((blk, blk), BF16), -1)
    qmap = lambda b, p, qt, kt: (b, qt[p], 0)
    kmap = lambda b, p, qt, kt: (b, kt[p], 0)
    grid_spec = pltpu.PrefetchScalarGridSpec(
        num_scalar_prefetch=2, grid=(n, qt.shape[0]),
        in_specs=[pl.BlockSpec((1, blk, 2 * W_SB), qmap), pl.BlockSpec((1, blk, W_SB), kmap),
                  pl.BlockSpec((1, blk, W_SB), kmap), _const_spec(u.shape)],
        out_specs=pl.BlockSpec((1, blk, W_SB), qmap),
        scratch_shapes=[pltpu.VMEM((H_SB, blk, LANES), F32)] * 2)
    return pl.pallas_call(
        functools.partial(_sb_kernel, blk=blk), grid_spec=grid_spec,
        out_shape=jax.ShapeDtypeStruct((n, t, W_SB), BF16),
        compiler_params=_attn_params(), name="sb_attn",
    )(qt, kt, qx, k, v, u)


def _diff_call(qx, k, v, bias, lam_params, gain2, lam_init):
    n, t, _ = k.shape
    blk = _attn_block(t)
    qt, kt = _causal_pairs(t // blk)
    qmap = lambda b, p, qt, kt: (b, qt[p], 0)
    kmap = lambda b, p, qt, kt: (b, kt[p], 0)
    grid_spec = pltpu.PrefetchScalarGridSpec(
        num_scalar_prefetch=2, grid=(n, qt.shape[0]),
        in_specs=[pl.BlockSpec((1, blk, 4 * W_DIFF), qmap), pl.BlockSpec((1, blk, W_DIFF), kmap),
                  pl.BlockSpec((1, blk, W_DIFF), kmap),
                  pl.BlockSpec((1, H_DIFF, blk, blk),
                               lambda b, p, qt, kt: (jnp.minimum(qt[p] - kt[p], 2), 0, 0, 0)),
                  _const_spec(lam_params.shape), _const_spec(gain2.shape)],
        out_specs=pl.BlockSpec((1, blk, W_DIFF), qmap),
        scratch_shapes=[pltpu.VMEM((2 * H_DIFF, blk, LANES), F32)] * 3)
    return pl.pallas_call(
        functools.partial(_diff_kernel, blk=blk, lam_init=lam_init), grid_spec=grid_spec,
        out_shape=jax.ShapeDtypeStruct((n, t, W_DIFF), BF16),
        compiler_params=_attn_params(), name="diff_attn",
    )(qt, kt, qx, k, v, bias, lam_params, gain2)


DEC_ROWS = 8
QROWS = 32
PAGES_PER_STEP = 4


def _stack_queries(x, n_groups, out_ref, lane_of_group):
    out_ref[...] = jnp.zeros_like(out_ref)
    x = jnp.where(lax.broadcasted_iota(jnp.int32, x.shape, 0) < 4, x, 0.0)
    for g in range(n_groups):
        a = x[:, 2 * LANES * g:2 * LANES * g + LANES]
        b = x[:, 2 * LANES * g + LANES:2 * LANES * (g + 1)]
        lo = lane_of_group(g)
        out_ref[8 * g:8 * (g + 1), lo:lo + LANES] = a + pltpu.roll(b, 4, axis=0)


def _pad_keys(x, rows):
    x = x.astype(F32)
    return jnp.concatenate([x, jnp.zeros((rows - x.shape[0], x.shape[1]), F32)], axis=0).astype(BF16)


def _head_lanes(col, n_heads):
    lane = lax.broadcasted_iota(jnp.int32, (col.shape[0], LANES), 1)
    out = jnp.zeros((col.shape[0], LANES), F32)
    for h in range(n_heads):
        out = jnp.where((lane >= 4 * h) & (lane < 4 * (h + 1)), col[:, h:h + 1], out)
    return out


def _suffix_t(b_t, tri, carry):
    hi, mid, lo = _split3(b_t)
    g = _dot(tri, hi) + _dot(tri, mid) + _dot(tri, lo) + carry
    return g.T, g[0:1, :] + b_t[0:1, :]


def _decode_kernel(pt_ref,
                   qxa_ref, kna_ref, vna_ref, lfn_ref, qxb_ref, knb_ref, vnb_ref, qxc_ref, knc_ref, vnc_ref,
                   u_ref, tri_ref, rb_ref, lp_ref, gain_ref, *rest, n_pages, past_len, lam_init):
    pp = PAGES_PER_STEP
    fox_pages = rest[0:pp]
    lf_pages = rest[pp:2 * pp]
    sb_pages = rest[2 * pp:3 * pp]
    df_pages = rest[3 * pp:4 * pp]
    oa_ref, ob_ref, oc_ref = rest[4 * pp:4 * pp + 3]
    (qa_s, qb_s, qc_s, ma_s, la_s, acca_s, gcar_s, qoff_s,
     carb_s, accb_s, mc_s, lc_s, accc_s) = rest[4 * pp + 3:]
    step = pl.program_id(1)
    page = fox_pages[0].shape[1]
    row = lax.broadcasted_iota(jnp.int32, (QROWS, LANES), 0)
    col = lax.broadcasted_iota(jnp.int32, (QROWS, LANES), 1)
    q_idx = row % 4
    u = u_ref[...]
    tri = tri_ref[...]

    def fox_update(s, v):
        m_prev = ma_s[...]
        m_new = jnp.maximum(m_prev, jnp.max(s, axis=-1, keepdims=True))
        alpha = jnp.exp(m_prev - m_new)
        p = jnp.exp(s - m_new)
        la_s[...] = alpha * la_s[...] + jnp.sum(p, axis=-1, keepdims=True)
        acca_s[...] = alpha[:, :1] * acca_s[...] + _dot(p.astype(BF16), v)
        ma_s[...] = m_new

    def sb_update(z, v, valid):
        sp = jnp.maximum(z, 0.0) + jnp.log(1.0 + jnp.exp(-jnp.abs(z)))
        if valid is not None:
            sp = jnp.where(valid, sp, 0.0)
        later = _dot(sp.astype(BF16), u)
        carry = carb_s[...]
        w = jnp.exp((z - sp) - later - carry)
        if valid is not None:
            w = jnp.where(valid, w, 0.0)
        accb_s[...] = accb_s[...] + _dot(w.astype(BF16), v)
        carb_s[...] = carry + (later[:, :1] + sp[:, :1])

    def diff_update(s, v):
        m_prev = mc_s[...]
        m_new = jnp.maximum(m_prev, jnp.max(s, axis=-1, keepdims=True))
        alpha = jnp.exp(m_prev - m_new)
        p = jnp.exp(s - m_new)
        lc_s[...] = alpha * lc_s[...] + jnp.sum(p, axis=-1, keepdims=True)
        accc_s[...] = alpha[:, :1] * accc_s[...] + _dot(p.astype(BF16), v)
        mc_s[...] = m_new

    def diff_bias(n):
        return jnp.concatenate([_bucket_bias(n[8 * h:8 * (h + 1)], rb_ref, h) for h in range(H_DIFF)], axis=0)

    @pl.when(step == 0)
    def _():
        _stack_queries(qxa_ref[0].astype(F32), H_FOX // 2, qa_s, lambda g: LANES * g)
        _stack_queries(qxb_ref[0].astype(F32), H_SB // 2, qb_s, lambda g: LANES * g)
        _stack_queries(qxc_ref[0].astype(F32), H_DIFF, qc_s, lambda g: LANES * (g // 2))
        for ref in (la_s, acca_s, carb_s, accb_s, lc_s, accc_s):
            ref[...] = jnp.zeros_like(ref)
        ma_s[...] = jnp.full_like(ma_s, NEG)
        mc_s[...] = jnp.full_like(mc_s, NEG)

        lf_new = lfn_ref[0]
        tok = lax.broadcasted_iota(jnp.int32, lf_new.shape, 0)
        lf_new = jnp.where(tok < 4, lf_new, 0.0)
        b_t = jnp.concatenate([_head_lanes(lf_new, H_FOX), jnp.zeros((page - DEC_ROWS, LANES), F32)], axis=0)
        g_t, total = _suffix_t(b_t, tri, jnp.zeros((1, LANES), F32))
        g_t = g_t[:QROWS]
        qoff = -jnp.sum(jnp.where(col == q_idx, g_t, 0.0), axis=-1, keepdims=True)
        qoff_s[...] = jnp.broadcast_to(qoff, qoff_s.shape)
        gcar_s[...] = jnp.broadcast_to(total, gcar_s.shape)

        incl = col <= q_idx
        s = _nt_dot(qa_s[...].astype(BF16), _pad_keys(kna_ref[0], page)) + g_t + qoff
        fox_update(jnp.where(incl, s, NEG), _pad_keys(vna_ref[0], page))

        z = _nt_dot(qb_s[...].astype(BF16), _pad_keys(knb_ref[0], page))
        sb_update(z, _pad_keys(vnb_ref[0], page), col < q_idx)

        s = _nt_dot(qc_s[...].astype(BF16), _pad_keys(knc_ref[0], page))
        s = s + diff_bias(jnp.maximum(q_idx - col, 0))
        diff_update(jnp.where(incl, s, NEG), _pad_keys(vnc_ref[0], page))

    for i in range(pp):
        kv = fox_pages[i][0]
        b_t = _head_lanes(lf_pages[i][0], H_FOX)
        g_t, total = _suffix_t(b_t, tri, gcar_s[0:1, :])
        s = _nt_dot(qa_s[...].astype(BF16), kv[:, :W_FOX].astype(BF16))
        s = s + g_t[:QROWS] + qoff_s[...]
        fox_update(s, kv[:, W_FOX:].astype(BF16))
        gcar_s[...] = jnp.broadcast_to(total, gcar_s.shape)

        kv = sb_pages[i][0]
        sb_update(_nt_dot(qb_s[...].astype(BF16), kv[:, :W_SB].astype(BF16)), kv[:, W_SB:].astype(BF16), None)

        kv = df_pages[i][0]
        s = _nt_dot(qc_s[...].astype(BF16), kv[:, :W_DIFF].astype(BF16))
        if i == 0:
            near = jnp.where(step == 0, 1.0, 0.0)
            s = s + near * diff_bias(jnp.maximum(page + q_idx - col, 0))
        diff_update(s, kv[:, W_DIFF:].astype(BF16))

    @pl.when(step == pl.num_programs(1) - 1)
    def _():
        lane = lax.broadcasted_iota(jnp.int32, (DEC_ROWS, LANES), 1)
        left = lane < HEAD_DIM
        acca = acca_s[...] / la_s[:, :1]
        accb = accb_s[...]
        for g in range(H_FOX // 2):
            t = acca[8 * g:8 * (g + 1), LANES * g:LANES * (g + 1)]
            oa_ref[0, :, LANES * g:LANES * (g + 1)] = jnp.where(left, t, pltpu.roll(t, 4, axis=0)).astype(BF16)
            t = accb[8 * g:8 * (g + 1), LANES * g:LANES * (g + 1)]
            ob_ref[0, :, LANES * g:LANES * (g + 1)] = jnp.where(left, t, pltpu.roll(t, 4, axis=0)).astype(BF16)
        accc = accc_s[...] / lc_s[:, :1]
        lam = _lambda_value(lp_ref, lam_init)
        for j in range(H_DIFF // 2):
            ta = accc[8 * (2 * j):8 * (2 * j + 1), LANES * j:LANES * (j + 1)]
            tb = accc[8 * (2 * j + 1):8 * (2 * j + 2), LANES * j:LANES * (j + 1)]
            o1 = jnp.where(left, ta, tb)
            o2 = jnp.where(left, pltpu.roll(ta, 4, axis=0), pltpu.roll(tb, 4, axis=0))
            o = _diff_finish(o1, o2, lam, gain_ref[...], lam_init)
            oc_ref[0, :, LANES * j:LANES * (j + 1)] = o.astype(BF16)


def _decode_call(page_rows, new, caches, rel_bias, lam_params, gain2, lam_init):
    qxa, kna, vna, lfn, qxb, knb, vnb, qxc, knc, vnc = new
    fox, lf, sb, df = caches
    n_seq = qxa.shape[0]
    n_pages = page_rows.shape[0] // n_seq
    pp = PAGES_PER_STEP
    page = fox.shape[1]
    u = jnp.tril(jnp.ones((page, page), BF16), -1)
    tri = jnp.triu(jnp.ones((page, page), BF16), 1)
    seq = lambda w: pl.BlockSpec((1, DEC_ROWS, w), lambda b, s, pt: (b, 0, 0))

    def page_spec(w, i):
        return pl.BlockSpec((1, page, w), lambda b, s, pt: (pt[b * n_pages + n_pages - 1 - (s * pp + i)], 0, 0))

    in_specs = [seq(2 * W_FOX), seq(W_FOX), seq(W_FOX), seq(LANES), seq(2 * W_SB), seq(W_SB), seq(W_SB),
                seq(4 * W_DIFF), seq(W_DIFF), seq(W_DIFF),
                pl.BlockSpec(u.shape, lambda b, s, pt: (0, 0)), pl.BlockSpec(tri.shape, lambda b, s, pt: (0, 0)),
                pl.BlockSpec(memory_space=pltpu.SMEM),
                pl.BlockSpec(lam_params.shape, lambda b, s, pt: (0, 0)),
                pl.BlockSpec(gain2.shape, lambda b, s, pt: (0, 0))]
    in_specs += [page_spec(2 * W_FOX, i) for i in range(pp)]
    in_specs += [page_spec(H_FOX, i) for i in range(pp)]
    in_specs += [page_spec(2 * W_SB, i) for i in range(pp)]
    in_specs += [page_spec(2 * W_DIFF, i) for i in range(pp)]
    vm = lambda r, w: pltpu.VMEM((r, w), F32)
    grid_spec = pltpu.PrefetchScalarGridSpec(
        num_scalar_prefetch=1, grid=(n_seq, n_pages // pp), in_specs=in_specs,
        out_specs=[seq(W_FOX), seq(W_SB), seq(W_DIFF)],
        scratch_shapes=[vm(QROWS, W_FOX), vm(QROWS, W_SB), vm(QROWS, W_DIFF),
                        vm(QROWS, LANES), vm(QROWS, LANES), vm(QROWS, W_FOX), vm(8, LANES), vm(QROWS, LANES),
                        vm(QROWS, LANES), vm(QROWS, W_SB),
                        vm(QROWS, LANES), vm(QROWS, LANES), vm(QROWS, W_DIFF)])
    out = lambda w: jax.ShapeDtypeStruct((n_seq, DEC_ROWS, w), BF16)
    return pl.pallas_call(
        functools.partial(_decode_kernel, n_pages=n_pages, past_len=n_pages * page, lam_init=lam_init),
        grid_spec=grid_spec, out_shape=[out(W_FOX), out(W_SB), out(W_DIFF)],
        compiler_params=pltpu.CompilerParams(dimension_semantics=("arbitrary", "arbitrary"),
                                             vmem_limit_bytes=VMEM_LIMIT),
        name="decode_attn",
    )(page_rows, qxa, kna, vna, lfn, qxb, knb, vnb, qxc, knc, vnc, u, tri, rel_bias, lam_params, gain2,
      *([fox] * pp), *([lf] * pp), *([sb] * pp), *([df] * pp))


def _ffn_weights(w_gate, w_up, w_down):
    d, f = w_gate.shape
    nch = f // FF_CHUNK
    chunk_cols = lambda w: w.astype(BF16).reshape(d, nch, FF_CHUNK).transpose(1, 0, 2)
    return chunk_cols(w_gate), chunk_cols(w_up), w_down.astype(BF16).reshape(nch, FF_CHUNK, d)


def kernel(x_prompt, x_sample, cache_fox_kv, cache_fox_logf, cache_sb_kv, cache_diff_kv, page_table,
           norm_ffn1, ffn1_gate, ffn1_up, ffn1_down, norm_mix, w_in, b_forget,
           lambda_q1, lambda_k1, lambda_q2, lambda_k2, diff_norm, w_out, rel_bias,
           norm_ffn2, ffn2_gate, ffn2_up, ffn2_down, final_norm):
    depth = w_in.shape[0]
    n, t, d = x_prompt.shape
    n_seq, n_new, _ = x_sample.shape
    n_phys, page = cache_fox_kv.shape[1], cache_fox_kv.shape[2]

    xp = x_prompt.reshape(n * t, d)
    xs = jnp.pad(x_sample, ((0, 0), (0, DEC_ROWS - n_new), (0, 0))).reshape(n_seq * DEC_ROWS, d)

    caches = (cache_fox_kv.reshape(depth * n_phys, page, 2 * W_FOX),
              cache_fox_logf.reshape(depth * n_phys, page, H_FOX),
              cache_sb_kv.reshape(depth * n_phys, page, 2 * W_SB),
              cache_diff_kv.reshape(depth * n_phys, page, 2 * W_DIFF))
    bias_tiles = _bias_call(rel_bias, _attn_block(t))
    row2 = lambda v: v.reshape(1, -1)
    final_g = row2(final_norm)

    outs = [[] for _ in range(8)]
    for l in range(depth):
        lam_init = 0.8 - 0.6 * math.exp(-0.3 * l)
        lam_params = jnp.stack([lambda_q1[l], lambda_k1[l], lambda_q2[l], lambda_k2[l]])
        gain2 = row2(jnp.concatenate([diff_norm[l], diff_norm[l]]))
        ffn1 = _ffn_weights(ffn1_gate[l], ffn1_up[l], ffn1_down[l])
        ffn2 = _ffn_weights(ffn2_gate[l], ffn2_up[l], ffn2_down[l])
        wl = w_in[l].astype(BF16)
        o0, o1, o2, o3 = W_FOX * 3, W_FOX * 3 + H_FOX, W_FOX * 3 + H_FOX + W_SB * 3, wl.shape[1]
        wa, wb, wc = wl[:, :o0], wl[:, o1:o2], wl[:, o2:o3]
        wf = jnp.pad(wl[:, o0:o1], ((0, 0), (0, LANES - H_FOX)))
        bf = row2(jnp.pad(b_forget[l], (0, LANES - H_FOX)))
        wo = w_out[l].astype(BF16)
        woa, wob, woc = wo[:W_FOX], wo[W_FOX:W_FOX + W_SB], wo[W_FOX + W_SB:]
        g_mix = row2(norm_mix[l])
        last = l == depth - 1

        xp = _ffn_call(xp, row2(norm_ffn1[l]), *ffn1)
        xs = _ffn_call(xs, row2(norm_ffn1[l]), *ffn1)

        (qxa, ka, va, kva, logf, cf, qxb, kb, vb, kvb, qxc, kc, vc, kvc) = _proj_call(
            xp.reshape(n, t, d), g_mix, wa, wf, bf, wb, wc)
        cf_t = jnp.swapaxes(cf[:, :, :8], 1, 2)
        oa = _fox_call(qxa, ka, va, cf, cf_t)
        ob = _sb_call(qxb, kb, vb)
        oc = _diff_call(qxc, kc, vc, bias_tiles, lam_params, gain2, lam_init)
        flat = lambda a: a.reshape(n * t, a.shape[-1])
        xp = _mix_ffn_call(xp, flat(oa), flat(ob), flat(oc), woa, wob, woc,
                           row2(norm_ffn2[l]), *ffn2, final_g, last)

        new = [a.reshape(n_seq, DEC_ROWS, a.shape[-1])
               for a in _proj_call(xs.reshape(1, n_seq * DEC_ROWS, d), g_mix, wa, wf, bf, wb, wc)]
        (sqxa, ska, sva, skva, slogf, _, sqxb, skb, svb, skvb, sqxc, skc, svc, skvc) = new
        page_rows = (page_table + l * n_phys).reshape(-1).astype(jnp.int32)
        soa, sob, soc = _decode_call(page_rows, (sqxa, ska, sva, slogf, sqxb, skb, svb, sqxc, skc, svc),
                                     caches, rel_bias, lam_params, gain2, lam_init)
        sflat = lambda a: a.reshape(n_seq * DEC_ROWS, a.shape[-1])
        xs = _mix_ffn_call(xs, sflat(soa), sflat(sob), sflat(soc), woa, wob, woc,
                           row2(norm_ffn2[l]), *ffn2, final_g, last)

        outs[0].append(kva.reshape(n, t, 2, H_FOX, HEAD_DIM))
        outs[1].append(skva[:, :n_new].reshape(n_seq, n_new, 2, H_FOX, HEAD_DIM))
        outs[2].append(logf[:, :, :H_FOX])
        outs[3].append(slogf[:, :n_new, :H_FOX])
        outs[4].append(kvb.reshape(n, t, 2, H_SB, HEAD_DIM))
        outs[5].append(skvb[:, :n_new].reshape(n_seq, n_new, 2, H_SB, HEAD_DIM))
        outs[6].append(kvc.reshape(n, t, 2, H_DIFF, HEAD_DIM))
        outs[7].append(skvc[:, :n_new].reshape(n_seq, n_new, 2, H_DIFF, HEAD_DIM))

    y_prompt = xp.reshape(n, t, d)
    y_sample = xs.reshape(n_seq, DEC_ROWS, d)[:, :n_new]
    return (y_prompt, y_sample) + tuple(jnp.stack(o) for o in outs)
```

```python
import functools
import math

import jax
import jax.numpy as jnp
from jax import lax
from jax.experimental import pallas as pl
from jax.experimental.pallas import tpu as pltpu

F32 = jnp.float32
BF16 = jnp.bfloat16

HEAD_DIM = 64
H_FOX = 6
H_SB = 6
H_DIFF = 4
DIFF_DIM = HEAD_DIM // 2
N_BUCKETS = 32
MAX_DISTANCE = 128
EPS = 1e-6

LANES = 128
SUBLANES = 8
W_FOX = H_FOX * HEAD_DIM
W_SB = H_SB * HEAD_DIM
W_DIFF = H_DIFF * HEAD_DIM
FF_CHUNK = 256
MXU_TILE = 256
NEG = -1e30
VMEM_LIMIT = 56 * 1024 * 1024
SB_DONE = 120.0


def _rms(x, g):
    ms = jnp.mean(x * x, axis=-1, keepdims=True)
    return x * lax.rsqrt(ms + EPS) * g


def _nt_dot(a, b):
    return lax.dot_general(a, b, (((1,), (1,)), ((), ())), preferred_element_type=F32)


def _dot(a, b):
    return jnp.dot(a, b, preferred_element_type=F32)


def _const_spec(shape):
    nd = len(shape)
    return pl.BlockSpec(shape, lambda *_: (0,) * nd, pipeline_mode=pl.Buffered(1))


def _softplus(z):
    return jnp.maximum(z, 0.0) + jnp.log(1.0 + jnp.exp(-jnp.abs(z)))


def _swiglu(x, g_ref, wg_ref, wu_ref, wd_ref):
    h = _rms(x, g_ref[...]).astype(BF16)
    acc = jnp.zeros_like(x)
    for c in range(wg_ref.shape[0]):
        gate = _dot(h, wg_ref[c])
        up = _dot(h, wu_ref[c])
        a = (gate * jax.nn.sigmoid(gate) * up).astype(BF16)
        acc = acc + _dot(a, wd_ref[c])
    return x + 0.5 * acc


def _ffn_kernel(x_ref, g_ref, wg_ref, wu_ref, wd_ref, o_ref):
    o_ref[...] = _swiglu(x_ref[...], g_ref, wg_ref, wu_ref, wd_ref)


def _mix_ffn_kernel(x_ref, oa_ref, ob_ref, oc_ref, woa_ref, wob_ref, woc_ref,
                    g_ref, wg_ref, wu_ref, wd_ref, fg_ref, o_ref, *, final):
    x = x_ref[...]
    x = x + _dot(oa_ref[...], woa_ref[...]) + _dot(ob_ref[...], wob_ref[...]) + _dot(oc_ref[...], woc_ref[...])
    y = _swiglu(x, g_ref, wg_ref, wu_ref, wd_ref)
    if final:
        y = _rms(y, fg_ref[...])
    o_ref[...] = y


def _token_tile(rows):
    return 512 if rows % 512 == 0 else rows


def _ffn_call(x, g, wg, wu, wd):
    rows, d = x.shape
    tm = _token_tile(rows)
    row_spec = pl.BlockSpec((tm, d), lambda i: (i, 0))
    return pl.pallas_call(
        _ffn_kernel,
        grid=(rows // tm,),
        in_specs=[row_spec, _const_spec(g.shape), _const_spec(wg.shape), _const_spec(wu.shape),
                  _const_spec(wd.shape)],
        out_specs=row_spec,
        out_shape=jax.ShapeDtypeStruct((rows, d), F32),
        compiler_params=pltpu.CompilerParams(dimension_semantics=("arbitrary",),
                                             vmem_limit_bytes=VMEM_LIMIT),
        name="ffn",
    )(x, g, wg, wu, wd)


def _mix_ffn_call(x, oa, ob, oc, woa, wob, woc, g, wg, wu, wd, fg, final):
    rows, d = x.shape
    tm = _token_tile(rows)
    row = lambda w: pl.BlockSpec((tm, w), lambda i: (i, 0))
    return pl.pallas_call(
        functools.partial(_mix_ffn_kernel, final=final),
        grid=(rows // tm,),
        in_specs=[row(d), row(W_FOX), row(W_SB), row(W_DIFF),
                  _const_spec(woa.shape), _const_spec(wob.shape), _const_spec(woc.shape),
                  _const_spec(g.shape), _const_spec(wg.shape), _const_spec(wu.shape), _const_spec(wd.shape),
                  _const_spec(fg.shape)],
        out_specs=row(d),
        out_shape=jax.ShapeDtypeStruct((rows, d), F32),
        compiler_params=pltpu.CompilerParams(dimension_semantics=("arbitrary",),
                                             vmem_limit_bytes=VMEM_LIMIT),
        name="mix_ffn",
    )(x, oa, ob, oc, woa, wob, woc, g, wg, wu, wd, fg)


def _split3(x):
    hi = x.astype(BF16)
    r1 = x - hi.astype(F32)
    mid = r1.astype(BF16)
    lo = (r1 - mid.astype(F32)).astype(BF16)
    return hi, mid, lo


def _dot3(a, x):
    hi, mid, lo = _split3(x)
    return _dot(a, hi) + _dot(a, mid) + _dot(a, lo)


def _pair_masked_store(q, n_heads, seg, scale, out_ref):
    lane = lax.broadcasted_iota(jnp.int32, (q.shape[0], LANES), 1)
    per_pair = LANES // seg
    for j in range(n_heads * HEAD_DIM // LANES):
        q2 = q[:, LANES * j:LANES * (j + 1)] * scale
        for t in range(per_pair):
            keep = (lane >= seg * t) & (lane < seg * (t + 1))
            v = per_pair * j + t
            out_ref[0, :, LANES * v:LANES * (v + 1)] = jnp.where(keep, q2, 0.0).astype(BF16)


def _head_copy_store(x, n_heads, out_ref, other):
    lane = lax.broadcasted_iota(jnp.int32, (x.shape[0], LANES), 1)
    for h in range(n_heads):
        j = h // 2
        own = (lane < HEAD_DIM) if h % 2 == 0 else (lane >= HEAD_DIM)
        fill = other(h, HEAD_DIM if h % 2 == 0 else 0)
        out_ref[0, :, LANES * h:LANES * (h + 1)] = jnp.where(own, x[:, LANES * j:LANES * (j + 1)], fill).astype(BF16)


def _lane_values(rows, first, vals):
    lane = lax.broadcasted_iota(jnp.int32, (rows, LANES), 1)
    out = jnp.zeros((rows, LANES), F32)
    for i, v in enumerate(vals):
        out = jnp.where(lane == first + i, v, out)
    return out


def _proj_kernel(x_ref, g_ref, wa_ref, wf_ref, bf_ref, wb_ref, wc_ref, *rest, prompt):
    if prompt:
        tri_ref, rest = rest[0], rest[1:]
    (qxa_ref, ka_ref, va_ref, kva_ref, logf_ref, qxb_ref, kb_ref, vb_ref, kvb_ref,
     qxc_ref, kc_ref, vc_ref, kvc_ref) = rest[:13]
    h = _rms(x_ref[0], g_ref[...]).astype(BF16)
    tm = h.shape[0]

    pa = _dot(h, wa_ref[...])
    kva_ref[0] = pa[:, W_FOX:]
    logf = jax.nn.log_sigmoid(_dot(h, wf_ref[...]) + bf_ref[...])
    logf_ref[0] = logf
    pb = _dot(h, wb_ref[...])
    _pair_masked_store(pb[:, :W_SB], H_SB, HEAD_DIM, HEAD_DIM ** -0.5, qxb_ref)
    kb_ref[0] = pb[:, W_SB:2 * W_SB].astype(BF16)
    vb_ref[0] = pb[:, 2 * W_SB:].astype(BF16)
    kvb_ref[0] = pb[:, W_SB:]
    pc = _dot(h, wc_ref[...])
    _pair_masked_store(pc[:, :W_DIFF], H_DIFF, DIFF_DIM, DIFF_DIM ** -0.5, qxc_ref)
    kc_ref[0] = pc[:, W_DIFF:2 * W_DIFF].astype(BF16)
    kvc_ref[0] = pc[:, W_DIFF:]

    if not prompt:
        _pair_masked_store(pa[:, :W_FOX], H_FOX, HEAD_DIM, HEAD_DIM ** -0.5, qxa_ref)
        ka_ref[0] = pa[:, W_FOX:2 * W_FOX].astype(BF16)
        va_ref[0] = pa[:, 2 * W_FOX:].astype(BF16)
        vc_ref[0] = pc[:, 2 * W_DIFF:].astype(BF16)
        return

    carry_ref = rest[13]

    @pl.when(pl.program_id(1) == 0)
    def _():
        carry_ref[...] = jnp.zeros_like(carry_ref)

    cf = _dot3(tri_ref[...], logf) + carry_ref[0:1, :]
    carry_ref[...] = jnp.broadcast_to(cf[tm - 1:, :], carry_ref.shape)

    def neg_cf_parts(hd, first):
        x = -cf[:, hd:hd + 1]
        hi = x.astype(BF16).astype(F32)
        mid = (x - hi).astype(BF16).astype(F32)
        return _lane_values(tm, first, [hi, mid, x - hi - mid])

    _head_copy_store(pa[:, :W_FOX] * HEAD_DIM ** -0.5, H_FOX, qxa_ref,
                     lambda hd, first: _lane_values(tm, first, [1.0, 1.0, 1.0]))
    _head_copy_store(pa[:, W_FOX:2 * W_FOX], H_FOX, ka_ref, neg_cf_parts)
    _head_copy_store(pa[:, 2 * W_FOX:], H_FOX, va_ref, lambda hd, first: 1.0)
    _head_copy_store(pc[:, 2 * W_DIFF:], H_DIFF, vc_ref, lambda hd, first: 1.0)


def _proj_call(x, g, wa, wf, bf, wb, wc, prompt):
    n, t, d = x.shape
    tm = _token_tile(t)
    blk = lambda w: pl.BlockSpec((1, tm, w), lambda b, i: (b, i, 0))
    out = lambda w, dt: jax.ShapeDtypeStruct((n, t, w), dt)
    wide = 2 if prompt else 1
    widths = [(2 * W_FOX, BF16), (wide * W_FOX, BF16), (wide * W_FOX, BF16), (2 * W_FOX, F32), (LANES, F32),
              (2 * W_SB, BF16), (W_SB, BF16), (W_SB, BF16), (2 * W_SB, F32),
              (4 * W_DIFF, BF16), (W_DIFF, BF16), (wide * W_DIFF, BF16), (2 * W_DIFF, F32)]
    args = [x, g, wa, wf, bf, wb, wc]
    if prompt:
        args.append(jnp.tril(jnp.ones((tm, tm), BF16)))
    return pl.pallas_call(
        functools.partial(_proj_kernel, prompt=prompt),
        grid=(n, t // tm),
        in_specs=[blk(d)] + [_const_spec(a.shape) for a in args[1:]],
        out_specs=[blk(w) for w, _ in widths],
        out_shape=[out(w, dt) for w, dt in widths],
        scratch_shapes=[pltpu.VMEM((SUBLANES, LANES), F32)] if prompt else [],
        compiler_params=pltpu.CompilerParams(dimension_semantics=("arbitrary", "arbitrary"),
                                             vmem_limit_bytes=VMEM_LIMIT),
        name="proj",
    )(*args)


def _causal_pairs(nq):
    qs, ks = [], []
    for q in range(nq):
        for k in range(q, -1, -1):
            qs.append(q)
            ks.append(k)
    return jnp.asarray(qs, jnp.int32), jnp.asarray(ks, jnp.int32)


def _delta(rows, cols):
    return lax.broadcasted_iota(jnp.int32, (rows, cols), 1) - lax.broadcasted_iota(jnp.int32, (rows, cols), 0)


def _online_softmax(slot, s, m_ref, acc_ref, vx):
    m_prev = m_ref[slot]
    m_new = jnp.maximum(m_prev, jnp.max(s, axis=-1, keepdims=True))
    alpha = jnp.exp(m_prev - m_new)
    p = jnp.exp(s - m_new[:, :1]).astype(BF16)
    acc_ref[slot] = alpha * acc_ref[slot] + _dot(p, vx)
    m_ref[slot] = m_new


def _normalised(acc_ref, a, b):
    xa, xb = acc_ref[a], acc_ref[b]
    left = lax.broadcasted_iota(jnp.int32, xa.shape, 1) < HEAD_DIM
    num = jnp.where(left, xa, xb)
    den = jnp.where(left, pltpu.roll(xa, HEAD_DIM, axis=1), pltpu.roll(xb, HEAD_DIM, axis=1))
    return num / den


def _fox_kernel(qt_ref, kt_ref, qx_ref, kx_ref, vx_ref, o_ref, m_ref, acc_ref, *, blk):
    p_id = pl.program_id(1)
    qi, ki = qt_ref[p_id], kt_ref[p_id]

    @pl.when(ki == qi)
    def _():
        m_ref[...] = jnp.full_like(m_ref, NEG)
        acc_ref[...] = jnp.zeros_like(acc_ref)

    def heads(diag):
        causal = _delta(blk, blk) <= 0 if diag else None
        for h in range(H_FOX):
            c = slice(LANES * h, LANES * (h + 1))
            s = _nt_dot(qx_ref[0, :, c], kx_ref[0, :, c])
            if diag:
                s = jnp.where(causal, s, NEG)
            _online_softmax(h, s, m_ref, acc_ref, vx_ref[0, :, c])

    pl.when(ki == qi)(lambda: heads(True))
    pl.when(ki < qi)(lambda: heads(False))

    @pl.when(ki == 0)
    def _():
        for j in range(H_FOX // 2):
            o_ref[0, :, LANES * j:LANES * (j + 1)] = _normalised(acc_ref, 2 * j, 2 * j + 1).astype(BF16)


def _sb_kernel(qt_ref, kt_ref, qx_ref, k_ref, v_ref, u_ref, o_ref, carry_ref, acc_ref, live_ref, *, blk):
    p_id = pl.program_id(1)
    qi, ki = qt_ref[p_id], kt_ref[p_id]
    hb = u_ref.shape[0]

    @pl.when(ki == qi)
    def _():
        carry_ref[...] = jnp.zeros_like(carry_ref)
        acc_ref[...] = jnp.zeros_like(acc_ref)

    def head(h, diag):
        j = h // 2
        q = qx_ref[0, :, LANES * h:LANES * (h + 1)]
        for half in reversed(range(blk // hb)):
            c0 = half * hb
            z = _nt_dot(q, k_ref[0, c0:c0 + hb, LANES * j:LANES * (j + 1)])
            sp = _softplus(z)
            if diag:
                strict = _delta(blk, hb) < -c0
                sp = jnp.where(strict, sp, 0.0)
            later = _dot(sp.astype(BF16), u_ref[...])
            carry = carry_ref[h]
            w = jnp.exp((z - sp) - later - carry[:, :1])
            if diag:
                w = jnp.where(strict, w, 0.0)
            acc_ref[h] = acc_ref[h] + _dot(w.astype(BF16), v_ref[0, c0:c0 + hb, LANES * j:LANES * (j + 1)])
            carry_ref[h] = carry + (later[:, :1] + sp[:, :1])
        live_ref[h] = (jnp.min(carry_ref[h]) < SB_DONE).astype(jnp.int32)

    for h in range(H_SB):
        pl.when(ki == qi)(functools.partial(head, h, True))
        pl.when(jnp.logical_and(ki < qi, live_ref[h] > 0))(functools.partial(head, h, False))

    @pl.when(ki == 0)
    def _():
        lane = lax.broadcasted_iota(jnp.int32, (blk, LANES), 1)
        for j in range(H_SB // 2):
            o = jnp.where(lane < HEAD_DIM, acc_ref[2 * j], acc_ref[2 * j + 1])
            o_ref[0, :, LANES * j:LANES * (j + 1)] = o.astype(BF16)


def _lambda_value(lp_ref, lam_init):
    lp = lp_ref[...]
    s1 = jnp.sum(lp[0:1] * lp[1:2], axis=-1, keepdims=True)
    s2 = jnp.sum(lp[2:3] * lp[3:4], axis=-1, keepdims=True)
    return jnp.exp(s1) - jnp.exp(s2) + lam_init


def _diff_finish(o1, o2, lam, gain, lam_init):
    lane = lax.broadcasted_iota(jnp.int32, o1.shape, 1)
    left = lane < HEAD_DIM
    o = o1 - lam * o2
    sq = o * o
    ss_all = jnp.sum(sq, axis=-1, keepdims=True)
    ss_left = jnp.sum(jnp.where(left, sq, 0.0), axis=-1, keepdims=True)
    ms = jnp.where(left, ss_left, ss_all - ss_left) * (1.0 / HEAD_DIM)
    return o * lax.rsqrt(ms + EPS) * gain * (1.0 - lam_init)


def _diff_kernel(qt_ref, kt_ref, qx_ref, k_ref, vx_ref, bias_ref, lp_ref, gain_ref, o_ref,
                 m_ref, acc_ref, *, blk, lam_init):
    p_id = pl.program_id(1)
    qi, ki = qt_ref[p_id], kt_ref[p_id]

    @pl.when(ki == qi)
    def _():
        m_ref[...] = jnp.full_like(m_ref, NEG)
        acc_ref[...] = jnp.zeros_like(acc_ref)

    def maps(near):
        causal = _delta(blk, blk) <= (qi - ki) * blk if near else None
        for t in range(2 * H_DIFF):
            h = t // 2
            j = h // 2
            s = _nt_dot(qx_ref[0, :, LANES * t:LANES * (t + 1)], k_ref[0, :, LANES * j:LANES * (j + 1)])
            if near:
                s = jnp.where(causal, s + bias_ref[0, h], NEG)
            _online_softmax(t, s, m_ref, acc_ref, vx_ref[0, :, LANES * h:LANES * (h + 1)])

    pl.when(qi - ki <= 1)(lambda: maps(True))
    pl.when(qi - ki > 1)(lambda: maps(False))

    @pl.when(ki == 0)
    def _():
        lam = _lambda_value(lp_ref, lam_init)
        for j in range(H_DIFF // 2):
            o1 = _normalised(acc_ref, 4 * j, 4 * j + 2)
            o2 = _normalised(acc_ref, 4 * j + 1, 4 * j + 3)
            o = _diff_finish(o1, o2, lam, gain_ref[...], lam_init)
            o_ref[0, :, LANES * j:LANES * (j + 1)] = o.astype(BF16)


def _bucket_bias(n, rb_ref, h):
    max_exact = N_BUCKETS // 2
    nf = jnp.maximum(n, 1).astype(F32)
    large = max_exact + (jnp.log(nf / max_exact) / math.log(MAX_DISTANCE / max_exact)
                         * (N_BUCKETS - max_exact)).astype(jnp.int32)
    bucket = jnp.where(n < max_exact, n, jnp.minimum(large, N_BUCKETS - 1))
    far = rb_ref[N_BUCKETS - 1, h]
    acc = jnp.zeros(n.shape, F32)
    for b in range(N_BUCKETS - 1):
        acc = jnp.where(bucket == b, rb_ref[b, h] - far, acc)
    return acc


def _bias_kernel(rb_ref, o_ref, *, blk):
    off = pl.program_id(0)
    h = pl.program_id(1)
    o_ref[0, 0] = _bucket_bias(jnp.maximum(off * blk - _delta(blk, blk), 0), rb_ref, h)


def _bias_call(rel_bias, blk):
    return pl.pallas_call(
        functools.partial(_bias_kernel, blk=blk),
        grid=(2, H_DIFF),
        in_specs=[pl.BlockSpec(memory_space=pltpu.SMEM)],
        out_specs=pl.BlockSpec((1, 1, blk, blk), lambda o, h: (o, h, 0, 0)),
        out_shape=jax.ShapeDtypeStruct((2, H_DIFF, blk, blk), F32),
        name="bias_tiles",
    )(rel_bias)


def _attn_block(t):
    return 512 if t % 512 == 0 else t


def _attn_params():
    return pltpu.CompilerParams(dimension_semantics=("arbitrary", "arbitrary"), vmem_limit_bytes=VMEM_LIMIT)


def _fox_call(qx, kx, vx):
    n, t, w = qx.shape
    blk = _attn_block(t)
    qt, kt = _causal_pairs(t // blk)
    qmap = lambda b, p, qt, kt: (b, qt[p], 0)
    kmap = lambda b, p, qt, kt: (b, kt[p], 0)
    grid_spec = pltpu.PrefetchScalarGridSpec(
        num_scalar_prefetch=2, grid=(n, qt.shape[0]),
        in_specs=[pl.BlockSpec((1, blk, w), qmap), pl.BlockSpec((1, blk, w), kmap), pl.BlockSpec((1, blk, w), kmap)],
        out_specs=pl.BlockSpec((1, blk, W_FOX), qmap),
        scratch_shapes=[pltpu.VMEM((H_FOX, blk, LANES), F32)] * 2)
    return pl.pallas_call(
        functools.partial(_fox_kernel, blk=blk), grid_spec=grid_spec,
        out_shape=jax.ShapeDtypeStruct((n, t, W_FOX), BF16),
        compiler_params=_attn_params(), name="fox_attn",
    )(qt, kt, qx, kx, vx)


def _sb_call(qx, k, v):
    n, t, _ = k.shape
    blk = _attn_block(t)
    hb = min(blk, MXU_TILE)
    qt, kt = _causal_pairs(t // blk)
    u = jnp.tril(jnp.ones((hb, hb), BF16), -1)
    qmap = lambda b, p, qt, kt: (b, qt[p], 0)
    kmap = lambda b, p, qt, kt: (b, kt[p], 0)
    grid_spec = pltpu.PrefetchScalarGridSpec(
        num_scalar_prefetch=2, grid=(n, qt.shape[0]),
        in_specs=[pl.BlockSpec((1, blk, 2 * W_SB), qmap), pl.BlockSpec((1, blk, W_SB), kmap),
                  pl.BlockSpec((1, blk, W_SB), kmap), _const_spec(u.shape)],
        out_specs=pl.BlockSpec((1, blk, W_SB), qmap),
        scratch_shapes=[pltpu.VMEM((H_SB, blk, LANES), F32)] * 2 + [pltpu.SMEM((SUBLANES,), jnp.int32)])
    return pl.pallas_call(
        functools.partial(_sb_kernel, blk=blk), grid_spec=grid_spec,
        out_shape=jax.ShapeDtypeStruct((n, t, W_SB), BF16),
        compiler_params=_attn_params(), name="sb_attn",
    )(qt, kt, qx, k, v, u)


def _diff_call(qx, k, vx, bias, lam_params, gain2, lam_init):
    n, t, _ = k.shape
    blk = _attn_block(t)
    qt, kt = _causal_pairs(t // blk)
    qmap = lambda b, p, qt, kt: (b, qt[p], 0)
    kmap = lambda b, p, qt, kt: (b, kt[p], 0)
    grid_spec = pltpu.PrefetchScalarGridSpec(
        num_scalar_prefetch=2, grid=(n, qt.shape[0]),
        in_specs=[pl.BlockSpec((1, blk, 4 * W_DIFF), qmap), pl.BlockSpec((1, blk, W_DIFF), kmap),
                  pl.BlockSpec((1, blk, 2 * W_DIFF), kmap),
                  pl.BlockSpec((1, H_DIFF, blk, blk),
                               lambda b, p, qt, kt: (jnp.minimum(qt[p] - kt[p], 1), 0, 0, 0)),
                  _const_spec(lam_params.shape), _const_spec(gain2.shape)],
        out_specs=pl.BlockSpec((1, blk, W_DIFF), qmap),
        scratch_shapes=[pltpu.VMEM((2 * H_DIFF, blk, LANES), F32)] * 2)
    return pl.pallas_call(
        functools.partial(_diff_kernel, blk=blk, lam_init=lam_init), grid_spec=grid_spec,
        out_shape=jax.ShapeDtypeStruct((n, t, W_DIFF), BF16),
        compiler_params=_attn_params(), name="diff_attn",
    )(qt, kt, qx, k, vx, bias, lam_params, gain2)


DEC_ROWS = 8
N_NEW = 4
QROWS = 32
PAGES_PER_STEP = 4
PAGE = 128
FLAT = PAGE * SUBLANES


def _lane_tile8(z):
    for k in (8, 16, 32, 64):
        z = z + pltpu.roll(z, k, axis=1)
    return z


def _logf_rows_kernel(pt_ref, *rest):
    pp = PAGES_PER_STEP
    pages, o_ref, xs_ref, carry_ref = rest[:pp], rest[pp], rest[pp + 1], rest[pp + 2]

    @pl.when(pl.program_id(1) == 0)
    def _():
        carry_ref[...] = jnp.zeros_like(carry_ref)

    xs_ref[...] = jnp.zeros_like(xs_ref)
    for i in range(pp):
        xs_ref[i:i + 1, :] = pages[i][0, 0]
    lane = lax.broadcasted_iota(jnp.int32, (SUBLANES, LANES), 1)
    sub = lax.broadcasted_iota(jnp.int32, (SUBLANES, LANES), 0)
    n_chunk = FLAT // LANES
    within, totals = [], []
    for v in range(n_chunk):
        x = xs_ref[:, LANES * v:LANES * (v + 1)]
        y = x
        for k in (8, 16, 32, 64):
            y = y + jnp.where(lane < LANES - k, pltpu.roll(y, LANES - k, axis=1), 0.0)
        within.append(y - x)
        totals.append(_lane_tile8(jnp.where(lane < SUBLANES, y, 0.0)))
    acc = jnp.zeros((SUBLANES, LANES), F32)
    for v in reversed(range(n_chunk)):
        within[v] = within[v] + acc
        acc = acc + totals[v]
    p = acc
    for k in (1, 2, 4):
        p = p + jnp.where(sub >= k, pltpu.roll(p, k, axis=0), 0.0)
    before = p - acc + carry_ref[...]
    for v in range(n_chunk):
        o_ref[0, 0, :, LANES * v:LANES * (v + 1)] = (within[v] + before)[:pp]
    carry_ref[...] = carry_ref[...] + jnp.broadcast_to(p[SUBLANES - 1:, :], carry_ref.shape)


def _logf_rows_call(page_table, lf8, layer):
    n_seq, n_pages = page_table.shape
    pp = PAGES_PER_STEP

    def page_spec(i):
        return pl.BlockSpec((1, 1, 1, FLAT), lambda b, s, pt: (layer, pt[b, n_pages - 1 - (s * pp + i)], 0, 0))

    grid_spec = pltpu.PrefetchScalarGridSpec(
        num_scalar_prefetch=1, grid=(n_seq, n_pages // pp),
        in_specs=[page_spec(i) for i in range(pp)],
        out_specs=pl.BlockSpec((1, 1, pp, FLAT), lambda b, s, pt: (b, s, 0, 0)),
        scratch_shapes=[pltpu.VMEM((SUBLANES, FLAT), F32), pltpu.VMEM((SUBLANES, LANES), F32)])
    return pl.pallas_call(
        _logf_rows_kernel, grid_spec=grid_spec,
        out_shape=jax.ShapeDtypeStruct((n_seq, n_pages // pp, pp, FLAT), F32),
        compiler_params=pltpu.CompilerParams(dimension_semantics=("arbitrary", "arbitrary")),
        name="logf_rows",
    )(page_table, *([lf8] * pp))


def _stack_queries(x, n_groups, out_ref, lane_of_group):
    out_ref[...] = jnp.zeros_like(out_ref)
    x = jnp.where(lax.broadcasted_iota(jnp.int32, x.shape, 0) < N_NEW, x, 0.0)
    for g in range(n_groups):
        a = x[:, 2 * LANES * g:2 * LANES * g + LANES]
        b = x[:, 2 * LANES * g + LANES:2 * LANES * (g + 1)]
        lo = lane_of_group(g)
        out_ref[8 * g:8 * (g + 1), lo:lo + LANES] = a + pltpu.roll(b, N_NEW, axis=0)


def _own_head(x, rows_per_pair):
    row = lax.broadcasted_iota(jnp.int32, (QROWS, LANES), 0)
    f = jnp.zeros((QROWS, LANES), F32)
    for g in range(x.shape[1] // LANES):
        f = jnp.where(row // rows_per_pair == g, x[:, LANES * g:LANES * (g + 1)], f)
    second = (row // (rows_per_pair // 2)) % 2 == 1
    return jnp.where(second, pltpu.roll(f, HEAD_DIM, axis=1), f)


def _pad_keys(x, rows):
    x = x.astype(F32)
    return jnp.concatenate([x, jnp.zeros((rows - x.shape[0], x.shape[1]), F32)], axis=0).astype(BF16)


def _head_lanes(col, n_heads):
    lane = lax.broadcasted_iota(jnp.int32, (col.shape[0], LANES), 1)
    out = jnp.zeros((col.shape[0], LANES), F32)
    for h in range(n_heads):
        out = jnp.where((lane >= N_NEW * h) & (lane < N_NEW * (h + 1)), col[:, h:h + 1], out)
    return out


def _flat_page(x):
    x = jnp.pad(x, ((0, 0), (0, SUBLANES - x.shape[1]), (0, LANES - x.shape[2])))
    return x.reshape(FLAT, LANES).astype(BF16)


def _dot3_right(x, b):
    hi, mid, lo = _split3(x)
    return _dot(hi, b) + _dot(mid, b) + _dot(lo, b)


def _decode_kernel(pt_ref,
                   qxa_ref, kna_ref, vna_ref, lfn_ref, qxb_ref, knb_ref, vnb_ref, qxc_ref, knc_ref, vnc_ref,
                   u128_ref, tri_ref, u256_ref, tile8_ref, rb_ref, lp_ref, gain_ref, lfrow_ref, *rest, lam_init):
    pp = PAGES_PER_STEP
    fox_pages, sb_pages, df_pages = rest[0:pp], rest[pp:2 * pp], rest[2 * pp:3 * pp]
    oa_ref, ob_ref, oc_ref = rest[3 * pp:3 * pp + 3]
    (qa_s, qb_s, qc_s, qfa_s, qfb_s, qfc_s, maska_s, keepa_s, maskc_s, nearc_s, newf_s, qoff_s,
     ma_s, la_s, acca_s, carb_s, accb_s, mc_s, lc_s, accc_s) = rest[3 * pp + 3:]
    step = pl.program_id(1)
    row = lax.broadcasted_iota(jnp.int32, (QROWS, LANES), 0)
    col = lax.broadcasted_iota(jnp.int32, (QROWS, LANES), 1)
    q_idx = row % N_NEW

    def softmax_update(s, v, m_s, l_s, acc_s, fold):
        m_prev = m_s[...]
        m_new = jnp.maximum(m_prev, jnp.max(s, axis=-1, keepdims=True))
        alpha = jnp.exp(m_prev - m_new)
        p = jnp.exp(s - m_new[:, :1])
        l_s[...] = alpha * l_s[...] + jnp.sum(p, axis=-1, keepdims=True)
        acc_s[...] = alpha * acc_s[...] + fold(_dot(p.astype(BF16), v))
        m_s[...] = m_new

    def diff_bias(n):
        return jnp.concatenate([_bucket_bias(n[8 * h:8 * (h + 1)], rb_ref, h) for h in range(H_DIFF)], axis=0)

    @pl.when(step == 0)
    def _():
        _stack_queries(qxa_ref[0].astype(F32), H_FOX // 2, qa_s, lambda g: LANES * g)
        _stack_queries(qxb_ref[0].astype(F32), H_SB // 2, qb_s, lambda g: LANES * g)
        _stack_queries(qxc_ref[0].astype(F32), H_DIFF, qc_s, lambda g: LANES * (g // 2))
        qfa_s[...] = _own_head(qa_s[...], 8)
        qfb_s[...] = _own_head(qb_s[...], 8)
        qfc_s[...] = _own_head(qc_s[...], 16)
        for ref in (la_s, acca_s, carb_s, accb_s, lc_s, accc_s):
            ref[...] = jnp.zeros_like(ref)
        ma_s[...] = jnp.full_like(ma_s, NEG)
        mc_s[...] = jnp.full_like(mc_s, NEG)

        frow = lax.broadcasted_iota(jnp.int32, (QROWS, FLAT), 0)
        fcol = lax.broadcasted_iota(jnp.int32, (QROWS, FLAT), 1)
        own_a = fcol % SUBLANES == frow // N_NEW
        own_c = fcol % SUBLANES == frow // (2 * N_NEW)
        maska_s[...] = jnp.where(own_a, 0.0, NEG)
        keepa_s[...] = jnp.where(own_a, 1.0, 0.0)
        maskc_s[...] = jnp.where(own_c, 0.0, NEG)
        nearc_s[...] = diff_bias(jnp.maximum(PAGE + frow % N_NEW - fcol // SUBLANES, 0))

        lf_new = lfn_ref[0]
        tok = lax.broadcasted_iota(jnp.int32, lf_new.shape, 0)
        hd = lax.broadcasted_iota(jnp.int32, lf_new.shape, 1)
        lf_new = jnp.where((tok < N_NEW) & (hd < H_FOX), lf_new, 0.0)
        b_t = jnp.concatenate([_head_lanes(lf_new, H_FOX), jnp.zeros((PAGE - DEC_ROWS, LANES), F32)], axis=0)
        g_t = _dot3(tri_ref[...], b_t).T[:QROWS]
        qoff = -jnp.sum(jnp.where(col == q_idx, g_t, 0.0), axis=-1, keepdims=True)
        qoff_s[...] = jnp.broadcast_to(qoff, qoff_s.shape)
        tot8 = jnp.broadcast_to(jnp.sum(lf_new, axis=0, keepdims=True), (SUBLANES, LANES))
        newf_s[...] = _dot3_right(tot8, tile8_ref[...])

        incl = col <= q_idx
        s = _nt_dot(qa_s[...].astype(BF16), _pad_keys(kna_ref[0], PAGE)) + g_t + qoff
        softmax_update(jnp.where(incl, s, NEG), _pad_keys(vna_ref[0], PAGE), ma_s, la_s, acca_s,
                       lambda o: _own_head(o, 8))

        z = _nt_dot(qb_s[...].astype(BF16), _pad_keys(knb_ref[0], PAGE))
        before = col < q_idx
        sp = jnp.where(before, _softplus(z), 0.0)
        later = _dot(sp.astype(BF16), u128_ref[...])
        w = jnp.where(before, jnp.exp((z - sp) - later), 0.0)
        accb_s[...] = _own_head(_dot(w.astype(BF16), _pad_keys(vnb_ref[0], PAGE)), 8)
        carb_s[...] = jnp.broadcast_to(later[:, :1] + sp[:, :1], carb_s.shape)

        s = _nt_dot(qc_s[...].astype(BF16), _pad_keys(knc_ref[0], PAGE))
        s = s + diff_bias(jnp.maximum(q_idx - col, 0))
        softmax_update(jnp.where(incl, s, NEG), _pad_keys(vnc_ref[0], PAGE), mc_s, lc_s, accc_s,
                       lambda o: _own_head(o, 16))

    same = lambda o: o
    keys = lambda pages, kv: jnp.concatenate([_flat_page(p[0, 0, :, kv]) for p in pages], axis=0)
    wide = lambda x: jnp.concatenate([x] * pp, axis=1)

    lf_after = jnp.concatenate([lfrow_ref[0, 0, i:i + 1, :] for i in range(pp)], axis=1) + wide(newf_s[0:1, :])
    s = _nt_dot(qfa_s[...].astype(BF16), keys(fox_pages, 0))
    softmax_update(s + lf_after + qoff_s[:, :1] + wide(maska_s[...]), keys(fox_pages, 1), ma_s, la_s, acca_s, same)

    z = _nt_dot(qfb_s[...].astype(BF16), keys(sb_pages, 0))
    sp = _softplus(z) * wide(keepa_s[...])
    spb = sp.astype(BF16)
    v = keys(sb_pages, 1)
    run = carb_s[...][:, :1]
    acc = accb_s[...]
    for i in range(pp):
        for b in reversed(range(FLAT // MXU_TILE)):
            c = slice(FLAT * i + MXU_TILE * b, FLAT * i + MXU_TILE * (b + 1))
            later = _dot(spb[:, c], u256_ref[...])
            w = jnp.exp((z[:, c] - sp[:, c]) - later - run + maska_s[:, MXU_TILE * b:MXU_TILE * (b + 1)])
            acc = acc + _dot(w.astype(BF16), v[c, :])
            run = run + (later[:, :1] + sp[:, c][:, :1])
    accb_s[...] = acc
    carb_s[...] = jnp.broadcast_to(run, carb_s.shape)

    near = jnp.concatenate([jnp.where(step == 0, 1.0, 0.0) * nearc_s[...],
                            jnp.zeros((QROWS, FLAT * (pp - 1)), F32)], axis=1)
    s = _nt_dot(qfc_s[...].astype(BF16), keys(df_pages, 0)) + wide(maskc_s[...]) + near
    softmax_update(s, keys(df_pages, 1), mc_s, lc_s, accc_s, same)

    @pl.when(step == pl.num_programs(1) - 1)
    def _():
        lane = lax.broadcasted_iota(jnp.int32, (DEC_ROWS, LANES), 1)
        left = lane < HEAD_DIM
        swap = lambda t: pltpu.roll(pltpu.roll(t, N_NEW, axis=0), HEAD_DIM, axis=1)
        acca = acca_s[...] / la_s[...]
        accb = accb_s[...]
        for g in range(H_FOX // 2):
            t = acca[8 * g:8 * (g + 1)]
            oa_ref[0, :, LANES * g:LANES * (g + 1)] = jnp.where(left, t, swap(t)).astype(BF16)
            t = accb[8 * g:8 * (g + 1)]
            ob_ref[0, :, LANES * g:LANES * (g + 1)] = jnp.where(left, t, swap(t)).astype(BF16)
        accc = accc_s[...] / lc_s[...]
        lam = _lambda_value(lp_ref, lam_init)
        for j in range(H_DIFF // 2):
            ta = accc[8 * (2 * j):8 * (2 * j + 1)]
            tb = pltpu.roll(accc[8 * (2 * j + 1):8 * (2 * j + 2)], HEAD_DIM, axis=1)
            o1 = jnp.where(left, ta, tb)
            o2 = jnp.where(left, pltpu.roll(ta, N_NEW, axis=0), pltpu.roll(tb, N_NEW, axis=0))
            o = _diff_finish(o1, o2, lam, gain_ref[...], lam_init)
            oc_ref[0, :, LANES * j:LANES * (j + 1)] = o.astype(BF16)


def _decode_call(page_table, layer, new, caches, lf_rows, rel_bias, lam_params, gain2, lam_init):
    qxa, kna, vna, lfn, qxb, knb, vnb, qxc, knc, vnc = new
    fox, sb, df = caches
    n_seq, n_pages = page_table.shape
    pp = PAGES_PER_STEP
    assert fox.shape[2] == PAGE and n_pages % pp == 0
    u128 = jnp.tril(jnp.ones((PAGE, PAGE), BF16), -1)
    tri = jnp.triu(jnp.ones((PAGE, PAGE), BF16), 1)
    u256 = jnp.tril(jnp.ones((MXU_TILE, MXU_TILE), BF16), -1)
    tile8 = (jnp.arange(LANES)[:, None] == (jnp.arange(FLAT) % SUBLANES)[None, :]).astype(BF16)
    seq = lambda w: pl.BlockSpec((1, DEC_ROWS, w), lambda b, s, pt: (b, 0, 0))
    const = lambda a: pl.BlockSpec(a.shape, lambda b, s, pt: (0,) * a.ndim)

    def page_spec(heads, i):
        return pl.BlockSpec((1, 1, PAGE, 2, heads, HEAD_DIM),
                            lambda b, s, pt: (layer, pt[b, n_pages - 1 - (s * pp + i)], 0, 0, 0, 0))

    in_specs = [seq(2 * W_FOX), seq(W_FOX), seq(W_FOX), seq(LANES), seq(2 * W_SB), seq(W_SB), seq(W_SB),
                seq(4 * W_DIFF), seq(W_DIFF), seq(W_DIFF),
                const(u128), const(tri), const(u256), const(tile8),
                pl.BlockSpec(memory_space=pltpu.SMEM), const(lam_params), const(gain2),
                pl.BlockSpec((1, 1, pp, FLAT), lambda b, s, pt: (b, s, 0, 0))]
    in_specs += [page_spec(H_FOX, i) for i in range(pp)]
    in_specs += [page_spec(H_SB, i) for i in range(pp)]
    in_specs += [page_spec(H_DIFF, i) for i in range(pp)]
    vm = lambda r, w: pltpu.VMEM((r, w), F32)
    grid_spec = pltpu.PrefetchScalarGridSpec(
        num_scalar_prefetch=1, grid=(n_seq, n_pages // pp), in_specs=in_specs,
        out_specs=[seq(W_FOX), seq(W_SB), seq(W_DIFF)],
        scratch_shapes=[vm(QROWS, W_FOX), vm(QROWS, W_SB), vm(QROWS, W_DIFF),
                        vm(QROWS, LANES), vm(QROWS, LANES), vm(QROWS, LANES),
                        vm(QROWS, FLAT), vm(QROWS, FLAT), vm(QROWS, FLAT), vm(QROWS, FLAT),
                        vm(SUBLANES, FLAT), vm(QROWS, LANES),
                        vm(QROWS, LANES), vm(QROWS, LANES), vm(QROWS, LANES),
                        vm(QROWS, LANES), vm(QROWS, LANES),
                        vm(QROWS, LANES), vm(QROWS, LANES), vm(QROWS, LANES)])
    out = lambda w: jax.ShapeDtypeStruct((n_seq, DEC_ROWS, w), BF16)
    return pl.pallas_call(
        functools.partial(_decode_kernel, lam_init=lam_init),
        grid_spec=grid_spec, out_shape=[out(W_FOX), out(W_SB), out(W_DIFF)],
        compiler_params=pltpu.CompilerParams(dimension_semantics=("arbitrary", "arbitrary"),
                                             vmem_limit_bytes=VMEM_LIMIT),
        name="decode_attn",
    )(page_table, qxa, kna, vna, lfn, qxb, knb, vnb, qxc, knc, vnc, u128, tri, u256, tile8,
      rel_bias, lam_params, gain2, lf_rows, *([fox] * pp), *([sb] * pp), *([df] * pp))


def _ffn_weights(w_gate, w_up, w_down):
    d, f = w_gate.shape
    nch = f // FF_CHUNK
    chunk_cols = lambda w: w.astype(BF16).reshape(d, nch, FF_CHUNK).transpose(1, 0, 2)
    return chunk_cols(w_gate), chunk_cols(w_up), w_down.astype(BF16).reshape(nch, FF_CHUNK, d)


def kernel(x_prompt, x_sample, cache_fox_kv, cache_fox_logf, cache_sb_kv, cache_diff_kv, page_table,
           norm_ffn1, ffn1_gate, ffn1_up, ffn1_down, norm_mix, w_in, b_forget,
           lambda_q1, lambda_k1, lambda_q2, lambda_k2, diff_norm, w_out, rel_bias,
           norm_ffn2, ffn2_gate, ffn2_up, ffn2_down, final_norm):
    depth = w_in.shape[0]
    n, t, d = x_prompt.shape
    n_seq, n_new, _ = x_sample.shape
    n_phys, page = cache_fox_kv.shape[1], cache_fox_kv.shape[2]
    assert n_new == N_NEW and page == PAGE

    xp = x_prompt.reshape(n * t, d)
    xs = jnp.pad(x_sample, ((0, 0), (0, DEC_ROWS - n_new), (0, 0))).reshape(n_seq * DEC_ROWS, d)

    caches = (cache_fox_kv, cache_sb_kv, cache_diff_kv)
    lf8 = jnp.pad(cache_fox_logf, ((0, 0), (0, 0), (0, 0), (0, SUBLANES - H_FOX))).reshape(depth, n_phys, 1, FLAT)
    page_table = page_table.astype(jnp.int32)
    bias_tiles = _bias_call(rel_bias, _attn_block(t))
    row2 = lambda v: v.reshape(1, -1)
    final_g = row2(final_norm)

    outs = [[] for _ in range(8)]
    for l in range(depth):
        lam_init = 0.8 - 0.6 * math.exp(-0.3 * l)
        lam_params = jnp.stack([lambda_q1[l], lambda_k1[l], lambda_q2[l], lambda_k2[l]])
        gain2 = row2(jnp.concatenate([diff_norm[l], diff_norm[l]]))
        ffn1 = _ffn_weights(ffn1_gate[l], ffn1_up[l], ffn1_down[l])
        ffn2 = _ffn_weights(ffn2_gate[l], ffn2_up[l], ffn2_down[l])
        wl = w_in[l].astype(BF16)
        o0, o1, o2, o3 = W_FOX * 3, W_FOX * 3 + H_FOX, W_FOX * 3 + H_FOX + W_SB * 3, wl.shape[1]
        wa, wb, wc = wl[:, :o0], wl[:, o1:o2], wl[:, o2:o3]
        wf = jnp.pad(wl[:, o0:o1], ((0, 0), (0, LANES - H_FOX)))
        bf = row2(jnp.pad(b_forget[l], (0, LANES - H_FOX)))
        wo = w_out[l].astype(BF16)
        woa, wob, woc = wo[:W_FOX], wo[W_FOX:W_FOX + W_SB], wo[W_FOX + W_SB:]
        g_mix = row2(norm_mix[l])
        last = l == depth - 1

        xp = _ffn_call(xp, row2(norm_ffn1[l]), *ffn1)
        xs = _ffn_call(xs, row2(norm_ffn1[l]), *ffn1)

        (qxa, kxa, vxa, kva, logf, qxb, kb, vb, kvb, qxc, kc, vxc, kvc) = _proj_call(
            xp.reshape(n, t, d), g_mix, wa, wf, bf, wb, wc, True)
        oa = _fox_call(qxa, kxa, vxa)
        ob = _sb_call(qxb, kb, vb)
        oc = _diff_call(qxc, kc, vxc, bias_tiles, lam_params, gain2, lam_init)
        flat = lambda a: a.reshape(n * t, a.shape[-1])
        xp = _mix_ffn_call(xp, flat(oa), flat(ob), flat(oc), woa, wob, woc,
                           row2(norm_ffn2[l]), *ffn2, final_g, last)

        new = [a.reshape(n_seq, DEC_ROWS, a.shape[-1])
               for a in _proj_call(xs.reshape(1, n_seq * DEC_ROWS, d), g_mix, wa, wf, bf, wb, wc, False)]
        (sqxa, ska, sva, skva, slogf, sqxb, skb, svb, skvb, sqxc, skc, svc, skvc) = new
        lf_rows = _logf_rows_call(page_table, lf8, l)
        soa, sob, soc = _decode_call(page_table, l, (sqxa, ska, sva, slogf, sqxb, skb, svb, sqxc, skc, svc),
                                     caches, lf_rows, rel_bias, lam_params, gain2, lam_init)
        sflat = lambda a: a.reshape(n_seq * DEC_ROWS, a.shape[-1])
        xs = _mix_ffn_call(xs, sflat(soa), sflat(sob), sflat(soc), woa, wob, woc,
                           row2(norm_ffn2[l]), *ffn2, final_g, last)

        outs[0].append(kva.reshape(n, t, 2, H_FOX, HEAD_DIM))
        outs[1].append(skva[:, :n_new].reshape(n_seq, n_new, 2, H_FOX, HEAD_DIM))
        outs[2].append(logf[:, :, :H_FOX])
        outs[3].append(slogf[:, :n_new, :H_FOX])
        outs[4].append(kvb.reshape(n, t, 2, H_SB, HEAD_DIM))
        outs[5].append(skvb[:, :n_new].reshape(n_seq, n_new, 2, H_SB, HEAD_DIM))
        outs[6].append(kvc.reshape(n, t, 2, H_DIFF, HEAD_DIM))
        outs[7].append(skvc[:, :n_new].reshape(n_seq, n_new, 2, H_DIFF, HEAD_DIM))

    y_prompt = xp.reshape(n, t, d)
    y_sample = xs.reshape(n_seq, DEC_ROWS, d)[:, :n_new]
    return (y_prompt, y_sample) + tuple(jnp.stack(o) for o in outs)
```
